```python
import math
import jax, jax.numpy as jnp
from jax import lax
import numpy as np

D_MODEL = 1024
BATCH = 32
SEQ = 2048
DEPTH = 2
DEC_BATCH = 2
DEC_SEQ = 8192
PAST_LEN = 128

MIX_WIDTH = D_MODEL
FOURIER_WIDTH = D_MODEL // 4
FOURIER_GROUPS = 4
FOURIER_GROUP_DIM = FOURIER_WIDTH // FOURIER_GROUPS
ATTN_WIDTH = MIX_WIDTH - FOURIER_WIDTH
DIFF_HEAD_DIM = 64
DIFF_V_DIM = 2 * DIFF_HEAD_DIM
N_DIFF_HEADS = ATTN_WIDTH // DIFF_V_DIM
QK_WIDTH = N_DIFF_HEADS * 2 * DIFF_HEAD_DIM
IN_PROJ_WIDTH = FOURIER_WIDTH + 2 * QK_WIDTH + ATTN_WIDTH
ROT_DIM = DIFF_HEAD_DIM // 4
ROPE_THETA = 500000.0
Q_BLOCK = 128
FF_DENSE = 2816
N_EXPERTS = 8
TOP_K = 2
FF_EXPERT = 3584
N_DENSE_LAYERS = (DEPTH + 1) // 2
N_MOE_LAYERS = DEPTH // 2
EPS = 1e-5

kernel_name = 'hybrid_fnet_diffattn_encoder'


def rmsnorm(x, g):
    xf = x.astype(jnp.float32)
    y = xf * lax.rsqrt(jnp.mean(xf * xf, axis=-1, keepdims=True) + EPS)
    return (y * g.astype(jnp.float32)).astype(x.dtype)


def swiglu(t, w_gate_up, w_down):
    gu = t @ w_gate_up
    g, u = jnp.split(gu, 2, axis=-1)
    return (jax.nn.silu(g) * u) @ w_down


def fourier_mix(u):
    B, S, _ = u.shape
    ug = u.astype(jnp.float32).reshape(B, S, FOURIER_GROUPS, FOURIER_GROUP_DIM)
    f = jnp.fft.fft2(ug, axes=(1, 3), norm='ortho')
    return jnp.real(f).reshape(B, S, FOURIER_WIDTH).astype(u.dtype)


def rope_partial(t, pos):
    inv = ROPE_THETA ** (-jnp.arange(0, ROT_DIM, 2, dtype=jnp.float32) / ROT_DIM)
    ang = pos.astype(jnp.float32)[:, None] * inv[None, :]
    cos = jnp.cos(ang)[None, :, None, None, :]
    sin = jnp.sin(ang)[None, :, None, None, :]
    tr = t[..., :ROT_DIM].astype(jnp.float32)
    x1, x2 = tr[..., :ROT_DIM // 2], tr[..., ROT_DIM // 2:]
    rot = jnp.concatenate([x1 * cos - x2 * sin, x2 * cos + x1 * sin], axis=-1).astype(t.dtype)
    return jnp.concatenate([rot, t[..., ROT_DIM:]], axis=-1)


def diff_attention(q, k, v, lam, subln_g, lam_init):
    B, S = q.shape[0], q.shape[1]
    nb = S // Q_BLOCK
    scale = DIFF_HEAD_DIM ** -0.5
    qb = q.reshape(B, nb, Q_BLOCK, N_DIFF_HEADS, 2, DIFF_HEAD_DIM).transpose(1, 0, 2, 3, 4, 5)

    def block(qi):
        s = jnp.einsum('bqhcd,bkhcd->bhcqk', qi, k,
                       preferred_element_type=jnp.float32) * scale
        p = jax.nn.softmax(s, axis=-1)
        w = p[:, :, 0] - lam * p[:, :, 1]
        return jnp.einsum('bhqk,bkhv->bqhv', w.astype(v.dtype), v)

    o = lax.map(block, qb)
    o = o.transpose(1, 0, 2, 3, 4).reshape(B, S, N_DIFF_HEADS, DIFF_V_DIM)
    o = rmsnorm(o, subln_g) * (1.0 - lam_init)
    return o.reshape(B, S, ATTN_WIDTH)


def moe_swiglu(h, router_w, e_gate_up, e_down):
    B, S, D = h.shape
    t = h.reshape(B * S, D)
    logits = (t @ router_w).astype(jnp.float32)
    vals, idx = lax.top_k(logits, TOP_K)
    gates = jax.nn.softmax(vals, axis=-1)
    comb = jnp.sum(jax.nn.one_hot(idx, N_EXPERTS, dtype=jnp.float32) * gates[..., None], axis=1)
    out = jnp.zeros_like(t)
    for e in range(N_EXPERTS):
        out = out + comb[:, e:e + 1].astype(t.dtype) * swiglu(t, e_gate_up[e], e_down[e])
    return out.reshape(B, S, D)


def trunk(x, norm_mix, w_in, lambda_qk, subln_gain, w_out, norm_ffn,
          ffn_w_gate_up, ffn_w_down, router_w, expert_w_gate_up, expert_w_down, final_norm):
    B, S, _ = x.shape
    pos = jnp.arange(S, dtype=jnp.int32)
    for i in range(DEPTH):
        lam_init = 0.8 - 0.6 * math.exp(-0.3 * i)
        h = rmsnorm(x, norm_mix[i])
        proj = h @ w_in[i]
        u_f, q, k, v = jnp.split(
            proj, [FOURIER_WIDTH, FOURIER_WIDTH + QK_WIDTH, FOURIER_WIDTH + 2 * QK_WIDTH], axis=-1)
        q = rope_partial(q.reshape(B, S, N_DIFF_HEADS, 2, DIFF_HEAD_DIM), pos)
        k = rope_partial(k.reshape(B, S, N_DIFF_HEADS, 2, DIFF_HEAD_DIM), pos)
        v = v.reshape(B, S, N_DIFF_HEADS, DIFF_V_DIM)
        lq = lambda_qk[i].astype(jnp.float32)
        lam = jnp.exp(jnp.sum(lq[0] * lq[1])) - jnp.exp(jnp.sum(lq[2] * lq[3])) + lam_init
        o_attn = diff_attention(q, k, v, lam, subln_gain[i], lam_init)
        o_f = fourier_mix(u_f)
        x = x + jnp.concatenate([o_f, o_attn], axis=-1) @ w_out[i]
        h = rmsnorm(x, norm_ffn[i])
        if i % 2 == 0:
            x = x + swiglu(h, ffn_w_gate_up[i // 2], ffn_w_down[i // 2])
        else:
            j = i // 2
            x = x + moe_swiglu(h, router_w[j], expert_w_gate_up[j], expert_w_down[j])
    return rmsnorm(x, final_norm)


def setup_inputs(seed: int = 0) -> dict:
    key = jax.random.key(seed)
    ks = jax.random.split(key, 16)
    f32 = jnp.float32
    nrm = lambda k, shp, s: jax.random.normal(k, shp, f32) * s
    return {
        'x_prompt': nrm(ks[0], (BATCH, SEQ, D_MODEL), 1.0),
        'x_sample': nrm(ks[1], (DEC_BATCH, DEC_SEQ, D_MODEL), 1.0),
        'norm_mix': 1.0 + nrm(ks[2], (DEPTH, D_MODEL), 0.01),
        'w_in': nrm(ks[3], (DEPTH, D_MODEL, IN_PROJ_WIDTH), D_MODEL ** -0.5),
        'lambda_qk': nrm(ks[4], (DEPTH, 4, DIFF_HEAD_DIM), 0.1),
        'subln_gain': 1.0 + nrm(ks[5], (DEPTH, DIFF_V_DIM), 0.01),
        'w_out': nrm(ks[6], (DEPTH, MIX_WIDTH, D_MODEL), MIX_WIDTH ** -0.5),
        'norm_ffn': 1.0 + nrm(ks[7], (DEPTH, D_MODEL), 0.01),
        'ffn_w_gate_up': nrm(ks[8], (N_DENSE_LAYERS, D_MODEL, 2 * FF_DENSE), D_MODEL ** -0.5),
        'ffn_w_down': nrm(ks[9], (N_DENSE_LAYERS, FF_DENSE, D_MODEL), FF_DENSE ** -0.5),
        'router_w': nrm(ks[10], (N_MOE_LAYERS, D_MODEL, N_EXPERTS), D_MODEL ** -0.5),
        'expert_w_gate_up': nrm(ks[11], (N_MOE_LAYERS, N_EXPERTS, D_MODEL, 2 * FF_EXPERT), D_MODEL ** -0.5),
        'expert_w_down': nrm(ks[12], (N_MOE_LAYERS, N_EXPERTS, FF_EXPERT, D_MODEL), FF_EXPERT ** -0.5),
        'final_norm': 1.0 + nrm(ks[13], (D_MODEL,), 0.01),
    }


def reference(x_prompt, x_sample, norm_mix, w_in, lambda_qk, subln_gain, w_out, norm_ffn,
              ffn_w_gate_up, ffn_w_down, router_w, expert_w_gate_up, expert_w_down, final_norm):
    y_prompt = trunk(x_prompt, norm_mix, w_in, lambda_qk, subln_gain, w_out, norm_ffn,
                     ffn_w_gate_up, ffn_w_down, router_w, expert_w_gate_up, expert_w_down, final_norm)
    y_sample = trunk(x_sample, norm_mix, w_in, lambda_qk, subln_gain, w_out, norm_ffn,
                     ffn_w_gate_up, ffn_w_down, router_w, expert_w_gate_up, expert_w_down, final_norm)
    return (y_prompt, y_sample)
```

```python
import functools
import math

import numpy as np
import jax
import jax.numpy as jnp
from jax import lax
from jax.experimental import pallas as pl
from jax.experimental.pallas import tpu as pltpu

F32 = jnp.float32
BF16 = jnp.bfloat16

D_MODEL = 1024
DEPTH = 2
FOURIER_WIDTH = 256
FOURIER_GROUPS = 4
FOURIER_GROUP_DIM = 64
ATTN_WIDTH = 768
DIFF_HEAD_DIM = 64
DIFF_V_DIM = 128
N_DIFF_HEADS = 6
QK_WIDTH = 768
IN_PROJ_WIDTH = 2560
ROT_DIM = 16
ROPE_THETA = 500000.0
FF_DENSE = 2816
N_EXPERTS = 8
FF_EXPERT = 3584
EPS = 1e-5

LANES = 128
SUBLANES = 8
VMEM_LIMIT_BYTES = 56 * 1024 * 1024

TOKEN_TILE = 512
ATTN_Q_TILE = 256
ATTN_K_TILE = 512
DFT_ROW_TILE = 1024
DFT_K_TILE = 2048
FFN_CHUNK = 256
EXPERT_TILE = 1024
EXPERT_CHUNK = 512


def _cparams(sem):
    return pltpu.CompilerParams(dimension_semantics=sem, vmem_limit_bytes=VMEM_LIMIT_BYTES)


def _silu_mul(g, u):
    return (g * (1.0 / (1.0 + jnp.exp(-g)))) * u


def _norm_inproj_kernel(x_ref, g_ref, w_ref, bd_ref, ct_ref, sa_ref, sb_ref, yf_ref, qkv_ref, h_scr):
    x = x_ref[...]
    ms = jnp.mean(x * x, axis=-1, keepdims=True)
    h_scr[...] = ((x * lax.rsqrt(ms + EPS)) * g_ref[...]).astype(BF16)
    hb = h_scr[...]
    uf = jnp.dot(hb, w_ref[:, 0:FOURIER_WIDTH], preferred_element_type=F32)
    yf_ref[...] = jnp.dot(uf.astype(BF16), bd_ref[...], preferred_element_type=F32).astype(BF16)
    ct, sa, sb = ct_ref[...], sa_ref[...], sb_ref[...]
    width = 2 * LANES
    for blk in range(2 * QK_WIDTH // width):
        c0 = FOURIER_WIDTH + blk * width
        t = jnp.dot(hb, w_ref[:, c0:c0 + width], preferred_element_type=F32)
        r = t * ct + pltpu.roll(t, width - ROT_DIM // 2, 1) * sa + pltpu.roll(t, ROT_DIM // 2, 1) * sb
        if blk < QK_WIDTH // width:
            r = r * (DIFF_HEAD_DIM ** -0.5)
        qkv_ref[:, blk * width:(blk + 1) * width] = r.astype(BF16)
    v0 = FOURIER_WIDTH + 2 * QK_WIDTH
    for blk in range(ATTN_WIDTH // width):
        t = jnp.dot(hb, w_ref[:, v0 + blk * width:v0 + (blk + 1) * width], preferred_element_type=F32)
        qkv_ref[:, 2 * QK_WIDTH + blk * width:2 * QK_WIDTH + (blk + 1) * width] = t.astype(BF16)


def _norm_inproj(x, g, w_bf, bd, ct, sa, sb, seq_p, n_p, seq_s):
    n = x.shape[0]
    tm = TOKEN_TILE
    tiles_p = n_p // tm
    per_p, per_s = seq_p // tm, seq_s // tm

    def pos_map(i):
        return (jnp.where(i < tiles_p, i % per_p, (i - tiles_p) % per_s), 0)

    tab = pl.BlockSpec((tm, 2 * LANES), pos_map)
    return pl.pallas_call(
        _norm_inproj_kernel,
        grid=(n // tm,),
        in_specs=[
            pl.BlockSpec((tm, D_MODEL), lambda i: (i, 0)),
            pl.BlockSpec((1, D_MODEL), lambda i: (0, 0)),
            pl.BlockSpec((D_MODEL, IN_PROJ_WIDTH), lambda i: (0, 0)),
            pl.BlockSpec((FOURIER_WIDTH, 2 * FOURIER_WIDTH), lambda i: (0, 0)),
            tab, tab, tab,
        ],
        out_specs=[
            pl.BlockSpec((tm, 2 * FOURIER_WIDTH), lambda i: (i, 0)),
            pl.BlockSpec((tm, 2 * QK_WIDTH + ATTN_WIDTH), lambda i: (i, 0)),
        ],
        out_shape=[
            jax.ShapeDtypeStruct((n, 2 * FOURIER_WIDTH), BF16),
            jax.ShapeDtypeStruct((n, 2 * QK_WIDTH + ATTN_WIDTH), BF16),
        ],
        scratch_shapes=[pltpu.VMEM((tm, D_MODEL), BF16)],
        compiler_params=_cparams(("arbitrary",)),
        name="norm_inproj",
    )(x, g, w_bf, bd, ct, sa, sb)


def _attn_kernel(lq_ref, q_ref, k_ref, v_ref, g_ref, o_ref, vt_scr, acc1_scr, acc2_scr, *, seq, lam_init):
    tq, tk = ATTN_Q_TILE, ATTN_K_TILE
    nk = seq // tk
    for j in range(nk):
        vt_scr[j] = v_ref[j * tk:(j + 1) * tk, :].astype(F32).T.astype(BF16)

    lq = lq_ref[...]
    lam = (jnp.exp(jnp.sum(lq[0:1, :] * lq[1:2, :], axis=1, keepdims=True))
           - jnp.exp(jnp.sum(lq[2:3, :] * lq[3:4, :], axis=1, keepdims=True)) + lam_init)
    gain = g_ref[...]
    first = lax.broadcasted_iota(jnp.int32, (1, DIFF_V_DIM), 1) < DIFF_HEAD_DIM
    nt_dims = (((1,), (1,)), ((), ()))

    def q_body(qi, carry):
        q = q_ref[pl.ds(pl.multiple_of(qi * tq, tq), tq), :]
        q1 = jnp.where(first, q, jnp.zeros_like(q))
        q2 = jnp.where(first, jnp.zeros_like(q), q)
        acc1_scr[...] = jnp.zeros_like(acc1_scr)
        acc2_scr[...] = jnp.zeros_like(acc2_scr)

        def k_body(kb, st):
            m1, l1, m2, l2 = st
            kblk = k_ref[pl.ds(pl.multiple_of(kb * tk, tk), tk), :]
            vt = vt_scr[kb]

            def one(qc, m, l, acc_scr):
                s = lax.dot_general(kblk, qc, nt_dims, preferred_element_type=F32)
                m_new = jnp.maximum(m, jnp.max(s, axis=0, keepdims=True))
                alpha = jnp.exp(m - m_new)
                p = jnp.exp(s - m_new)
                l_new = alpha * l + jnp.sum(p, axis=0, keepdims=True)
                acc_scr[...] = alpha * acc_scr[...] + jnp.dot(vt, p.astype(BF16), preferred_element_type=F32)
                return m_new, l_new

            m1, l1 = one(q1, m1, l1, acc1_scr)
            m2, l2 = one(q2, m2, l2, acc2_scr)
            return m1, l1, m2, l2

        neg = jnp.full((1, tq), -jnp.inf, F32)
        zero = jnp.zeros((1, tq), F32)
        m1, l1, m2, l2 = lax.fori_loop(0, nk, k_body, (neg, zero, neg, zero))
        o_t = acc1_scr[...] * (1.0 / l1) - lam * (acc2_scr[...] * (1.0 / l2))
        ms = jnp.mean(o_t * o_t, axis=0, keepdims=True)
        y_t = ((o_t * lax.rsqrt(ms + EPS)) * gain) * (1.0 - lam_init)
        o_ref[pl.ds(pl.multiple_of(qi * tq, tq), tq), :] = y_t.T.astype(BF16)
        return carry

    lax.fori_loop(0, seq // tq, q_body, 0)


def _attention(qkv, lq, gain_col, row0, nb, seq, lam_init):
    blk0 = row0 // seq
    nh = N_DIFF_HEADS
    kern = functools.partial(_attn_kernel, seq=seq, lam_init=lam_init)
    return pl.pallas_call(
        kern,
        grid=(nb, nh),
        in_specs=[
            pl.BlockSpec((4, DIFF_HEAD_DIM), lambda b, h: (0, 0)),
            pl.BlockSpec((seq, LANES), lambda b, h: (blk0 + b, h)),
            pl.BlockSpec((seq, LANES), lambda b, h: (blk0 + b, nh + h)),
            pl.BlockSpec((seq, LANES), lambda b, h: (blk0 + b, 2 * nh + h)),
            pl.BlockSpec((DIFF_V_DIM, 1), lambda b, h: (0, 0)),
        ],
        out_specs=pl.BlockSpec((seq, LANES), lambda b, h: (b, h)),
        out_shape=jax.ShapeDtypeStruct((nb * seq, ATTN_WIDTH), BF16),
        scratch_shapes=[
            pltpu.VMEM((seq // ATTN_K_TILE, DIFF_V_DIM, ATTN_K_TILE), BF16),
            pltpu.VMEM((DIFF_V_DIM, ATTN_Q_TILE), F32),
            pltpu.VMEM((DIFF_V_DIM, ATTN_Q_TILE), F32),
        ],
        compiler_params=_cparams(("arbitrary", "arbitrary")),
        name="diff_attention",
    )(lq, qkv, qkv, qkv, gain_col)


def _dft_kernel(c_ref, s_ref, y_ref, o_ref, acc_scr, *, nk, scale):
    k = pl.program_id(2)
    y = y_ref[...]
    part = (jnp.dot(c_ref[...], y[:, 0:FOURIER_WIDTH], preferred_element_type=F32)
            + jnp.dot(s_ref[...], y[:, FOURIER_WIDTH:2 * FOURIER_WIDTH], preferred_element_type=F32))

    @pl.when(k == 0)
    def _():
        acc_scr[...] = part

    @pl.when(k > 0)
    def _():
        acc_scr[...] += part

    @pl.when(k == nk - 1)
    def _():
        o_ref[...] = (acc_scr[...] * scale).astype(BF16)


def _fourier(yf, cmat, smat, row0, nb, seq):
    tm = min(seq, DFT_ROW_TILE)
    tk = min(seq, DFT_K_TILE)
    ni, nk = seq // tm, seq // tk
    yblk0 = row0 // tk
    kern = functools.partial(_dft_kernel, nk=nk, scale=1.0 / math.sqrt(seq))
    return pl.pallas_call(
        kern,
        grid=(ni, nb, nk),
        in_specs=[
            pl.BlockSpec((tm, tk), lambda i, b, k: (i, k)),
            pl.BlockSpec((tm, tk), lambda i, b, k: (i, k)),
            pl.BlockSpec((tk, 2 * FOURIER_WIDTH), lambda i, b, k: (yblk0 + b * nk + k, 0)),
        ],
        out_specs=pl.BlockSpec((tm, FOURIER_WIDTH), lambda i, b, k: (b * ni + i, 0)),
        out_shape=jax.ShapeDtypeStruct((nb * seq, FOURIER_WIDTH), BF16),
        scratch_shapes=[pltpu.VMEM((tm, FOURIER_WIDTH), F32)],
        compiler_params=_cparams(("arbitrary", "arbitrary", "arbitrary")),
        name="fourier_dft",
    )(cmat, smat, yf)


def _outproj_kernel(x_ref, of_ref, oa_ref, w_ref, g_ref, x1_ref, h_ref):
    y = (jnp.dot(of_ref[...], w_ref[0:FOURIER_WIDTH, :], preferred_element_type=F32)
         + jnp.dot(oa_ref[...], w_ref[FOURIER_WIDTH:D_MODEL, :], preferred_element_type=F32))
    x1 = x_ref[...] + y
    x1_ref[...] = x1
    ms = jnp.mean(x1 * x1, axis=-1, keepdims=True)
    h_ref[...] = ((x1 * lax.rsqrt(ms + EPS)) * g_ref[...]).astype(BF16)


def _outproj(x, o_f, o_a, w_bf, g):
    n = x.shape[0]
    tm = TOKEN_TILE
    return pl.pallas_call(
        _outproj_kernel,
        grid=(n // tm,),
        in_specs=[
            pl.BlockSpec((tm, D_MODEL), lambda i: (i, 0)),
            pl.BlockSpec((tm, FOURIER_WIDTH), lambda i: (i, 0)),
            pl.BlockSpec((tm, ATTN_WIDTH), lambda i: (i, 0)),
            pl.BlockSpec((D_MODEL, D_MODEL), lambda i: (0, 0)),
            pl.BlockSpec((1, D_MODEL), lambda i: (0, 0)),
        ],
        out_specs=[
            pl.BlockSpec((tm, D_MODEL), lambda i: (i, 0)),
            pl.BlockSpec((tm, D_MODEL), lambda i: (i, 0)),
        ],
        out_shape=[
            jax.ShapeDtypeStruct((n, D_MODEL), F32),
            jax.ShapeDtypeStruct((n, D_MODEL), BF16),
        ],
        compiler_params=_cparams(("arbitrary",)),
        name="outproj_norm",
    )(x, o_f, o_a, w_bf, g)


def _ffn_kernel(x1_ref, h_ref, wgu_ref, wd_ref, o_ref, acc_scr):
    hb = h_ref[...]
    fc = FFN_CHUNK
    for j in range(FF_DENSE // fc):
        g = jnp.dot(hb, wgu_ref[:, j * fc:(j + 1) * fc], preferred_element_type=F32)
        u = jnp.dot(hb, wgu_ref[:, FF_DENSE + j * fc:FF_DENSE + (j + 1) * fc], preferred_element_type=F32)
        c = jnp.dot(_silu_mul(g, u).astype(BF16), wd_ref[j * fc:(j + 1) * fc, :], preferred_element_type=F32)
        if j == 0:
            acc_scr[...] = c
        else:
            acc_scr[...] += c
    o_ref[...] = x1_ref[...] + acc_scr[...]


def _dense_ffn(x1, h, wgu_bf, wd_bf):
    n = x1.shape[0]
    tm = TOKEN_TILE
    return pl.pallas_call(
        _ffn_kernel,
        grid=(n // tm,),
        in_specs=[
            pl.BlockSpec((tm, D_MODEL), lambda i: (i, 0)),
            pl.BlockSpec((tm, D_MODEL), lambda i: (i, 0)),
            pl.BlockSpec((D_MODEL, 2 * FF_DENSE), lambda i: (0, 0)),
            pl.BlockSpec((FF_DENSE, D_MODEL), lambda i: (0, 0)),
        ],
        out_specs=pl.BlockSpec((tm, D_MODEL), lambda i: (i, 0)),
        out_shape=jax.ShapeDtypeStruct((n, D_MODEL), F32),
        scratch_shapes=[pltpu.VMEM((tm, D_MODEL), F32)],
        compiler_params=_cparams(("arbitrary",)),
        name="dense_swiglu",
    )(x1, h, wgu_bf, wd_bf)


def _router_kernel(x_ref, g_ref, rw_ref, rt_ref, cnt_ref, carry_scr):
    i = pl.program_id(0)
    tm = TOKEN_TILE

    @pl.when(i == 0)
    def _():
        carry_scr[...] = jnp.zeros_like(carry_scr)

    x = x_ref[...]
    ms = jnp.mean(x * x, axis=-1, keepdims=True)
    h = (x * lax.rsqrt(ms + EPS)) * g_ref[...]
    h_hi = h.astype(BF16)
    h_lo = (h - h_hi.astype(F32)).astype(BF16)
    rw = rw_ref[...]
    rw_hi = rw.astype(BF16)
    rw_lo = (rw - rw_hi.astype(F32)).astype(BF16)
    nt_dims = (((1,), (1,)), ((), ()))
    dg = lambda a, b: lax.dot_general(a, b, nt_dims, preferred_element_type=F32)
    logits = dg(rw_hi, h_hi) + (dg(rw_hi, h_lo) + dg(rw_lo, h_hi))

    eidx = lax.broadcasted_iota(jnp.int32, (N_EXPERTS, tm), 0)
    m1 = jnp.max(logits, axis=0, keepdims=True)
    i1 = jnp.min(jnp.where(logits == m1, eidx, N_EXPERTS), axis=0, keepdims=True)
    oh1 = eidx == i1
    rest = jnp.where(oh1, -jnp.inf, logits)
    m2 = jnp.max(rest, axis=0, keepdims=True)
    i2 = jnp.min(jnp.where(rest == m2, eidx, N_EXPERTS), axis=0, keepdims=True)
    oh2 = eidx == i2
    e = jnp.exp(m2 - m1)
    g1 = 1.0 / (1.0 + e)
    g2 = e / (1.0 + e)

    onehot = jnp.where(oh1 | oh2, 1.0, 0.0)
    upper = (lax.broadcasted_iota(jnp.int32, (tm, tm), 0)
             < lax.broadcasted_iota(jnp.int32, (tm, tm), 1))
    before = jnp.dot(onehot.astype(BF16), jnp.where(upper, 1.0, 0.0).astype(BF16),
                     preferred_element_type=F32)
    rank = before + carry_scr[:, 0:1]
    r1 = jnp.sum(jnp.where(oh1, rank, 0.0), axis=0, keepdims=True)
    r2 = jnp.sum(jnp.where(oh2, rank, 0.0), axis=0, keepdims=True)
    zero = jnp.zeros_like(g1)
    rt_ref[...] = jnp.concatenate(
        [i1.astype(F32), i2.astype(F32), r1, r2, g1, g2, zero, zero], axis=0)
    total = carry_scr[...] + jnp.sum(onehot, axis=1, keepdims=True)
    carry_scr[...] = total
    cnt_ref[...] = total


def _router(x1, g, rw_t):
    n = x1.shape[0]
    tm = TOKEN_TILE
    return pl.pallas_call(
        _router_kernel,
        grid=(n // tm,),
        in_specs=[
            pl.BlockSpec((tm, D_MODEL), lambda i: (i, 0)),
            pl.BlockSpec((1, D_MODEL), lambda i: (0, 0)),
            pl.BlockSpec((N_EXPERTS, D_MODEL), lambda i: (0, 0)),
        ],
        out_specs=[
            pl.BlockSpec((SUBLANES, tm), lambda i: (0, i)),
            pl.BlockSpec((N_EXPERTS, LANES), lambda i: (0, 0)),
        ],
        out_shape=[
            jax.ShapeDtypeStruct((SUBLANES, n), F32),
            jax.ShapeDtypeStruct((N_EXPERTS, LANES), F32),
        ],
        scratch_shapes=[pltpu.VMEM((N_EXPERTS, LANES), F32)],
        compiler_params=_cparams(("arbitrary",)),
        name="router_top2",
    )(x1, g, rw_t)


def _row_copy(src, src_row, dst, dst_row, sem):
    return pltpu.make_async_copy(src.at[pl.ds(src_row, 1)], dst.at[pl.ds(dst_row, 1)], sem)


def _dispatch_kernel(s1_ref, s2_ref, h_ref, xs_ref, rows_scr, sem):
    tm = TOKEN_TILE
    rows_scr[...] = h_ref[...].astype(F32)

    def issue(t, c):
        _row_copy(rows_scr, t, xs_ref, s1_ref[t], sem).start()
        _row_copy(rows_scr, t, xs_ref, s2_ref[t], sem).start()
        return c

    lax.fori_loop(0, tm, issue, 0)

    def drain(t, c):
        _row_copy(rows_scr, 0, xs_ref, 0, sem).wait()
        _row_copy(rows_scr, 0, xs_ref, 0, sem).wait()
        return c

    lax.fori_loop(0, tm, drain, 0)


def _dispatch(slot1, slot2, h, n_slots):
    n = h.shape[0]
    tm = TOKEN_TILE
    smem = lambda: pl.BlockSpec((tm,), lambda i: (i,), memory_space=pltpu.SMEM)
    return pl.pallas_call(
        _dispatch_kernel,
        grid=(n // tm,),
        in_specs=[smem(), smem(), pl.BlockSpec((tm, D_MODEL), lambda i: (i, 0))],
        out_specs=pl.BlockSpec(memory_space=pl.ANY),
        out_shape=jax.ShapeDtypeStruct((n_slots, D_MODEL), F32),
        scratch_shapes=[pltpu.VMEM((tm, D_MODEL), F32), pltpu.SemaphoreType.DMA(())],
        compiler_params=_cparams(("arbitrary",)),
        name="moe_dispatch",
    )(slot1, slot2, h)


def _expert_kernel(te_ref, nt_ref, tv_ref, xs_ref, wg_ref, wu_ref, wd_ref, ys_ref, xb_scr):
    i = pl.program_id(0)
    j = pl.program_id(1)

    @pl.when(i < nt_ref[0])
    def _():
        @pl.when(j == 0)
        def _():
            rows = lax.broadcasted_iota(jnp.int32, (EXPERT_TILE, 1), 0)
            xb_scr[...] = jnp.where(rows < tv_ref[i], xs_ref[...], 0.0).astype(BF16)

        xb = xb_scr[...]
        g = jnp.dot(xb, wg_ref[0], preferred_element_type=F32)
        u = jnp.dot(xb, wu_ref[0], preferred_element_type=F32)
        c = jnp.dot(_silu_mul(g, u).astype(BF16), wd_ref[0], preferred_element_type=F32)

        @pl.when(j == 0)
        def _():
            ys_ref[...] = c

        @pl.when(j > 0)
        def _():
            ys_ref[...] += c


def _experts(tile_expert, n_tiles, tile_valid, xs, wgu_bf, wd_bf):
    n_slots = xs.shape[0]
    te, fc = EXPERT_TILE, EXPERT_CHUNK
    nj = FF_EXPERT // fc
    max_tiles = n_slots // te

    def tile(i, nt):
        return jnp.minimum(i, nt[0] - 1)

    def chunk(i, j, nt):
        return jnp.where(i < nt[0], j, nj - 1)

    grid_spec = pltpu.PrefetchScalarGridSpec(
        num_scalar_prefetch=3,
        grid=(max_tiles, nj),
        in_specs=[
            pl.BlockSpec((te, D_MODEL), lambda i, j, e, nt, tv: (tile(i, nt), 0)),
            pl.BlockSpec((1, D_MODEL, fc), lambda i, j, e, nt, tv: (e[tile(i, nt)], 0, chunk(i, j, nt))),
            pl.BlockSpec((1, D_MODEL, fc), lambda i, j, e, nt, tv: (e[tile(i, nt)], 0, nj + chunk(i, j, nt))),
            pl.BlockSpec((1, fc, D_MODEL), lambda i, j, e, nt, tv: (e[tile(i, nt)], chunk(i, j, nt), 0)),
        ],
        out_specs=pl.BlockSpec((te, D_MODEL), lambda i, j, e, nt, tv: (tile(i, nt), 0)),
        scratch_shapes=[pltpu.VMEM((te, D_MODEL), BF16)],
    )
    return pl.pallas_call(
        _expert_kernel,
        grid_spec=grid_spec,
        out_shape=jax.ShapeDtypeStruct((n_slots, D_MODEL), F32),
        compiler_params=_cparams(("arbitrary", "arbitrary")),
        name="expert_swiglu",
    )(tile_expert, n_tiles, tile_valid, xs, wgu_bf, wgu_bf, wd_bf)


def _combine_kernel(s1_ref, s2_ref, rt_ref, x1_ref, g_ref, ys_ref, o_ref, a_scr, b_scr, sem):
    tm = TOKEN_TILE

    def issue(t, c):
        _row_copy(ys_ref, s1_ref[t], a_scr, t, sem).start()
        _row_copy(ys_ref, s2_ref[t], b_scr, t, sem).start()
        return c

    lax.fori_loop(0, tm, issue, 0)

    rt = jnp.concatenate([rt_ref[...], jnp.zeros((LANES - SUBLANES, tm), F32)], axis=0)
    rt_t = rt.T
    g1 = rt_t[:, 4:5]
    g2 = rt_t[:, 5:6]

    def drain(t, c):
        _row_copy(ys_ref, 0, a_scr, 0, sem).wait()
        _row_copy(ys_ref, 0, b_scr, 0, sem).wait()
        return c

    lax.fori_loop(0, tm, drain, 0)

    x2 = x1_ref[...] + (g1 * a_scr[...] + g2 * b_scr[...])
    ms = jnp.mean(x2 * x2, axis=-1, keepdims=True)
    o_ref[...] = (x2 * lax.rsqrt(ms + EPS)) * g_ref[...]


def _combine(slot1, slot2, rt, x1, g_final, ys):
    n = x1.shape[0]
    tm = TOKEN_TILE
    smem = lambda: pl.BlockSpec((tm,), lambda i: (i,), memory_space=pltpu.SMEM)
    return pl.pallas_call(
        _combine_kernel,
        grid=(n // tm,),
        in_specs=[
            smem(), smem(),
            pl.BlockSpec((SUBLANES, tm), lambda i: (0, i)),
            pl.BlockSpec((tm, D_MODEL), lambda i: (i, 0)),
            pl.BlockSpec((1, D_MODEL), lambda i: (0, 0)),
            pl.BlockSpec(memory_space=pl.ANY),
        ],
        out_specs=pl.BlockSpec((tm, D_MODEL), lambda i: (i, 0)),
        out_shape=jax.ShapeDtypeStruct((n, D_MODEL), F32),
        scratch_shapes=[
            pltpu.VMEM((tm, D_MODEL), F32),
            pltpu.VMEM((tm, D_MODEL), F32),
            pltpu.SemaphoreType.DMA(()),
        ],
        compiler_params=_cparams(("arbitrary",)),
        name="moe_combine_norm",
    )(slot1, slot2, rt, x1, g_final, ys)


def _rope_tables(seq_max):
    half = ROT_DIM // 2
    inv = ROPE_THETA ** (-jnp.arange(0, ROT_DIM, 2, dtype=F32) / ROT_DIM)
    ang = jnp.arange(seq_max, dtype=F32)[:, None] * inv[None, :]
    cos, sin = jnp.cos(ang), jnp.sin(ang)
    ones = jnp.ones((seq_max, DIFF_HEAD_DIM - ROT_DIM), F32)
    zeros = jnp.zeros((seq_max, DIFF_HEAD_DIM - ROT_DIM), F32)
    z8 = jnp.zeros((seq_max, half), F32)
    ct = jnp.concatenate([cos, cos, ones], axis=1)
    sa = jnp.concatenate([-sin, z8, zeros], axis=1)
    sb = jnp.concatenate([z8, sin, zeros], axis=1)
    reps = 2 * LANES // DIFF_HEAD_DIM
    return tuple(jnp.tile(t, (1, reps)) for t in (ct, sa, sb))


def _channel_dft_matrix():
    c = np.arange(FOURIER_GROUP_DIM)
    ang = 2.0 * np.pi * ((c[:, None] * c[None, :]) % FOURIER_GROUP_DIM) / FOURIER_GROUP_DIM
    scale = FOURIER_GROUP_DIM ** -0.5
    eye = np.eye(FOURIER_GROUPS)
    bd = np.concatenate([np.kron(eye, np.cos(ang) * scale), np.kron(eye, np.sin(ang) * scale)], axis=1)
    return jnp.asarray(bd, dtype=BF16)


def _position_dft_matrices(seq):
    lo = FOURIER_GROUP_DIM
    hi = seq // lo
    sp = jnp.arange(seq, dtype=jnp.int32)[None, :]
    a = jnp.arange(hi, dtype=jnp.int32)[:, None]
    b = jnp.arange(lo, dtype=jnp.int32)[:, None]
    w = 2.0 * math.pi / seq
    ang_a = ((a * lo * sp) % seq).astype(F32) * w
    ang_b = ((b * sp) % seq).astype(F32) * w
    ca, sa = jnp.cos(ang_a)[:, None, :], jnp.sin(ang_a)[:, None, :]
    cb, sb = jnp.cos(ang_b)[None, :, :], jnp.sin(ang_b)[None, :, :]
    cmat = (ca * cb - sa * sb).reshape(seq, seq).astype(BF16)
    smat = (-(sa * cb + ca * sb)).reshape(seq, seq).astype(BF16)
    return cmat, smat


def kernel(x_prompt, x_sample, norm_mix, w_in, lambda_qk, subln_gain, w_out, norm_ffn, ffn_w_gate_up,
           ffn_w_down, router_w, expert_w_gate_up, expert_w_down, final_norm):
    nb_p, seq_p, _ = x_prompt.shape
    nb_s, seq_s, _ = x_sample.shape
    n_p, n_s = nb_p * seq_p, nb_s * seq_s
    n = n_p + n_s
    assert seq_p % TOKEN_TILE == 0 and seq_s % TOKEN_TILE == 0
    assert n_p % seq_s == 0 and seq_p % ATTN_K_TILE == 0 and seq_s % ATTN_K_TILE == 0
    assert n_p % min(seq_s, DFT_K_TILE) == 0

    x = jnp.concatenate([x_prompt.reshape(n_p, D_MODEL), x_sample.reshape(n_s, D_MODEL)], axis=0)

    ct, sa, sb = _rope_tables(max(seq_p, seq_s))
    bd = _channel_dft_matrix()
    dft_p = _position_dft_matrices(seq_p)
    dft_s = dft_p if seq_s == seq_p else _position_dft_matrices(seq_s)

    rt = slot1 = slot2 = None
    for layer in range(DEPTH):
        lam_init = 0.8 - 0.6 * math.exp(-0.3 * layer)
        yf, qkv = _norm_inproj(x, norm_mix[layer][None, :], w_in[layer].astype(BF16), bd, ct, sa, sb,
                               seq_p, n_p, seq_s)
        gain_col = subln_gain[layer][:, None]
        o_a = jnp.concatenate([
            _attention(qkv, lambda_qk[layer], gain_col, 0, nb_p, seq_p, lam_init),
            _attention(qkv, lambda_qk[layer], gain_col, n_p, nb_s, seq_s, lam_init)], axis=0)
        o_f = jnp.concatenate([
            _fourier(yf, dft_p[0], dft_p[1], 0, nb_p, seq_p),
            _fourier(yf, dft_s[0], dft_s[1], n_p, nb_s, seq_s)], axis=0)
        x1, h = _outproj(x, o_f, o_a, w_out[layer].astype(BF16), norm_ffn[layer][None, :])
        if layer % 2 == 0:
            x = _dense_ffn(x1, h, ffn_w_gate_up[layer // 2].astype(BF16), ffn_w_down[layer // 2].astype(BF16))
        else:
            j = layer // 2
            rt, cnt = _router(x1, norm_ffn[layer][None, :], router_w[j].T)
            te = EXPERT_TILE
            counts = cnt[:, 0].astype(jnp.int32)
            padded = ((counts + te - 1) // te) * te
            ends = jnp.cumsum(padded)
            offs = ends - padded
            e1, e2 = rt[0].astype(jnp.int32), rt[1].astype(jnp.int32)
            slot1 = offs[e1] + rt[2].astype(jnp.int32)
            slot2 = offs[e2] + rt[3].astype(jnp.int32)
            n_slots = 2 * n + N_EXPERTS * te
            starts = jnp.arange(n_slots // te, dtype=jnp.int32) * te
            tile_expert = jnp.minimum(jnp.searchsorted(ends, starts, side="right"), N_EXPERTS - 1).astype(jnp.int32)
            tile_valid = jnp.clip(counts[tile_expert] - (starts - offs[tile_expert]), 0, te).astype(jnp.int32)
            n_tiles = (ends[-1:] // te).astype(jnp.int32)
            xs = _dispatch(slot1, slot2, h, n_slots)
            ys = _experts(tile_expert, n_tiles, tile_valid, xs,
                          expert_w_gate_up[j].astype(BF16), expert_w_down[j].astype(BF16))
            x = _combine(slot1, slot2, rt, x1, final_norm[None, :], ys)
    assert DEPTH % 2 == 0
    y_p = x[:n_p].reshape(nb_p, seq_p, D_MODEL)
    y_s = x[n_p:].reshape(nb_s, seq_s, D_MODEL)
    return (y_p, y_s)
```

```python
import functools
import math

import numpy as np
import jax
import jax.numpy as jnp
from jax import lax
from jax.experimental import pallas as pl
from jax.experimental.pallas import tpu as pltpu

F32 = jnp.float32
BF16 = jnp.bfloat16

D_MODEL = 1024
DEPTH = 2
FOURIER_WIDTH = 256
FOURIER_GROUPS = 4
FOURIER_GROUP_DIM = 64
ATTN_WIDTH = 768
DIFF_HEAD_DIM = 64
DIFF_V_DIM = 128
N_DIFF_HEADS = 6
QK_WIDTH = 768
IN_PROJ_WIDTH = 2560
ROT_DIM = 16
ROPE_THETA = 500000.0
FF_DENSE = 2816
N_EXPERTS = 8
FF_EXPERT = 3584
EPS = 1e-5

LANES = 128
SUBLANES = 8
VMEM_LIMIT_BYTES = 56 * 1024 * 1024

TOKEN_TILE = 512
ATTN_Q_TILE = 512
ATTN_K_TILE = 512
ATTN_K_SUPER = 2048
LOG2E = 1.4426950408889634
DFT_ROW_TILE = 1024
DFT_K_TILE = 2048
FFN_CHUNK = 256
EXPERT_TILE = 1024
EXPERT_CHUNK = 512


def _cparams(sem):
    return pltpu.CompilerParams(dimension_semantics=sem, vmem_limit_bytes=VMEM_LIMIT_BYTES)


def _aligned(start, multiple):
    return start if isinstance(start, int) else pl.multiple_of(start, multiple)


def _silu_mul(g, u):
    return (g * (1.0 / (1.0 + jnp.exp(-g)))) * u


def _norm_inproj_kernel(x_ref, g_ref, w_ref, bd_ref, ct_ref, sa_ref, sb_ref, yf_ref, qkv_ref, h_scr):
    x = x_ref[...]
    ms = jnp.mean(x * x, axis=-1, keepdims=True)
    h_scr[...] = ((x * lax.rsqrt(ms + EPS)) * g_ref[...]).astype(BF16)
    hb = h_scr[...]
    uf = jnp.dot(hb, w_ref[:, 0:FOURIER_WIDTH], preferred_element_type=F32)
    yf_ref[...] = jnp.dot(uf.astype(BF16), bd_ref[...], preferred_element_type=F32).astype(BF16)
    ct, sa, sb = ct_ref[...], sa_ref[...], sb_ref[...]
    width = 2 * LANES
    for blk in range(2 * QK_WIDTH // width):
        c0 = FOURIER_WIDTH + blk * width
        t = jnp.dot(hb, w_ref[:, c0:c0 + width], preferred_element_type=F32)
        r = t * ct + pltpu.roll(t, width - ROT_DIM // 2, 1) * sa + pltpu.roll(t, ROT_DIM // 2, 1) * sb
        if blk < QK_WIDTH // width:
            r = r * (DIFF_HEAD_DIM ** -0.5 * LOG2E)
        qkv_ref[:, blk * width:(blk + 1) * width] = r.astype(BF16)
    v0 = FOURIER_WIDTH + 2 * QK_WIDTH
    for blk in range(ATTN_WIDTH // width):
        t = jnp.dot(hb, w_ref[:, v0 + blk * width:v0 + (blk + 1) * width], preferred_element_type=F32)
        qkv_ref[:, 2 * QK_WIDTH + blk * width:2 * QK_WIDTH + (blk + 1) * width] = t.astype(BF16)


def _norm_inproj(x, g, w_bf, bd, ct, sa, sb, seq_p, n_p, seq_s):
    n = x.shape[0]
    tm = TOKEN_TILE
    tiles_p = n_p // tm
    per_p, per_s = seq_p // tm, seq_s // tm

    def pos_map(i):
        return (jnp.where(i < tiles_p, i % per_p, (i - tiles_p) % per_s), 0)

    tab = pl.BlockSpec((tm, 2 * LANES), pos_map)
    return pl.pallas_call(
        _norm_inproj_kernel,
        grid=(n // tm,),
        in_specs=[
            pl.BlockSpec((tm, D_MODEL), lambda i: (i, 0)),
            pl.BlockSpec((1, D_MODEL), lambda i: (0, 0)),
            pl.BlockSpec((D_MODEL, IN_PROJ_WIDTH), lambda i: (0, 0)),
            pl.BlockSpec((FOURIER_WIDTH, 2 * FOURIER_WIDTH), lambda i: (0, 0)),
            tab, tab, tab,
        ],
        out_specs=[
            pl.BlockSpec((tm, 2 * FOURIER_WIDTH), lambda i: (i, 0)),
            pl.BlockSpec((tm, 2 * QK_WIDTH + ATTN_WIDTH), lambda i: (i, 0)),
        ],
        out_shape=[
            jax.ShapeDtypeStruct((n, 2 * FOURIER_WIDTH), BF16),
            jax.ShapeDtypeStruct((n, 2 * QK_WIDTH + ATTN_WIDTH), BF16),
        ],
        scratch_shapes=[pltpu.VMEM((tm, D_MODEL), BF16)],
        compiler_params=_cparams(("arbitrary",)),
        name="norm_inproj",
    )(x, g, w_bf, bd, ct, sa, sb)


def _attn_kernel(lq_ref, q_ref, k_ref, v_ref, g_ref, o_ref, vt_scr, sa_scr, sb_scr, p_scr, acc_scr, *,
                 seq, lam_init):
    tq, tk = ATTN_Q_TILE, ATTN_K_TILE
    ks = min(seq, ATTN_K_SUPER)
    nsb, nkc = seq // ks, ks // tk
    groups = tk // SUBLANES
    n_units = (seq // tq) * nsb
    assert n_units % 2 == 0
    s_bufs = (sa_scr, sb_scr)
    for sb in range(nsb):
        for j in range(nkc):
            r0 = sb * ks + j * tk
            vt_scr[sb, :, j * tk:(j + 1) * tk] = v_ref[r0:r0 + tk, :].astype(F32).T.astype(BF16)

    lq = lq_ref[...]
    lam = (jnp.exp(jnp.sum(lq[0:1, :] * lq[1:2, :], axis=1, keepdims=True))
           - jnp.exp(jnp.sum(lq[2:3, :] * lq[3:4, :], axis=1, keepdims=True)) + lam_init)
    gain = g_ref[...]
    first = lax.broadcasted_iota(jnp.int32, (1, DIFF_V_DIM), 1) < DIFF_HEAD_DIM
    nt_dims = (((1,), (1,)), ((), ()))
    comps = (0, 1)

    def load_q(u):
        q = q_ref[pl.ds(_aligned((u // nsb) * tq, tq), tq), :]
        return (jnp.where(first, q, jnp.zeros_like(q)), jnp.where(first, jnp.zeros_like(q), q))

    def score_chunk(u, qc, s_scr, j, mx):
        kblk = k_ref[pl.ds(_aligned((u % nsb) * ks + j * tk, tk), tk), :]
        out = []
        for c in comps:
            s = lax.dot_general(kblk, qc[c], nt_dims, preferred_element_type=F32)
            s_scr[c, j] = s
            out.append(jnp.maximum(mx[c], jnp.max(s.reshape(groups, SUBLANES, tq), axis=0)))
        return out

    def exp_chunk(s_scr, j, m_new, ls):
        out = []
        for c in comps:
            p = jnp.exp2(s_scr[c, j] - m_new[c])
            out.append(ls[c] + jnp.sum(p.reshape(groups, SUBLANES, tq), axis=0))
            p_scr[c, j * tk:(j + 1) * tk, :] = p.astype(BF16)
        return out

    def unit(u, slot, st, with_next):
        mx, m, l = st[0:2], st[2:4], st[4:6]
        sb, qi = u % nsb, u // nsb
        m_new = [jnp.maximum(m[c], jnp.max(mx[c], axis=0, keepdims=True)) for c in comps]
        alpha = [jnp.exp2(m[c] - m_new[c]) for c in comps]
        ls = [jnp.zeros((SUBLANES, tq), F32) for _ in comps]
        mx_next = [jnp.full((SUBLANES, tq), -jnp.inf, F32) for _ in comps]
        qc_next = load_q(u + 1) if with_next else None
        for j in range(nkc):
            if with_next:
                mx_next = score_chunk(u + 1, qc_next, s_bufs[1 - slot], j, mx_next)
            ls = exp_chunk(s_bufs[slot], j, m_new, ls)
        l_new = [alpha[c] * l[c] + jnp.sum(ls[c], axis=0, keepdims=True) for c in comps]
        vt = vt_scr[sb]
        for c in comps:
            acc_scr[c] = alpha[c] * acc_scr[c] + jnp.dot(vt, p_scr[c], preferred_element_type=F32)

        last = jnp.asarray(sb == nsb - 1)

        @pl.when(last)
        def _():
            o_t = acc_scr[0] * (1.0 / l_new[0]) - lam * (acc_scr[1] * (1.0 / l_new[1]))
            ms = jnp.mean(o_t * o_t, axis=0, keepdims=True)
            y_t = ((o_t * lax.rsqrt(ms + EPS)) * gain) * (1.0 - lam_init)
            o_ref[pl.ds(_aligned(qi * tq, tq), tq), :] = y_t.T.astype(BF16)
            acc_scr[...] = jnp.zeros_like(acc_scr)

        m_out = [jnp.where(last, -jnp.inf, m_new[c]) for c in comps]
        l_out = [jnp.where(last, 0.0, l_new[c]) for c in comps]
        return (mx_next[0], mx_next[1], m_out[0], m_out[1], l_out[0], l_out[1])

    acc_scr[...] = jnp.zeros_like(acc_scr)
    mx0 = [jnp.full((SUBLANES, tq), -jnp.inf, F32) for _ in comps]
    qc0 = load_q(0)
    for j in range(nkc):
        mx0 = score_chunk(0, qc0, s_bufs[0], j, mx0)
    neg = jnp.full((1, tq), -jnp.inf, F32)
    zero = jnp.zeros((1, tq), F32)
    st = (mx0[0], mx0[1], neg, neg, zero, zero)

    def pair(i, st):
        st = unit(2 * i, 0, st, True)
        return unit(2 * i + 1, 1, st, True)

    st = lax.fori_loop(0, n_units // 2 - 1, pair, st)
    st = unit(n_units - 2, 0, st, True)
    unit(n_units - 1, 1, st, False)


def _attention(qkv, lq, gain_col, row0, nb, seq, lam_init):
    blk0 = row0 // seq
    nh = N_DIFF_HEADS
    tq, tk = ATTN_Q_TILE, ATTN_K_TILE
    ks = min(seq, ATTN_K_SUPER)
    kern = functools.partial(_attn_kernel, seq=seq, lam_init=lam_init)
    return pl.pallas_call(
        kern,
        grid=(nb, nh),
        in_specs=[
            pl.BlockSpec((4, DIFF_HEAD_DIM), lambda b, h: (0, 0)),
            pl.BlockSpec((seq, LANES), lambda b, h: (blk0 + b, h)),
            pl.BlockSpec((seq, LANES), lambda b, h: (blk0 + b, nh + h)),
            pl.BlockSpec((seq, LANES), lambda b, h: (blk0 + b, 2 * nh + h)),
            pl.BlockSpec((DIFF_V_DIM, 1), lambda b, h: (0, 0)),
        ],
        out_specs=pl.BlockSpec((seq, LANES), lambda b, h: (b, h)),
        out_shape=jax.ShapeDtypeStruct((nb * seq, ATTN_WIDTH), BF16),
        scratch_shapes=[
            pltpu.VMEM((seq // ks, DIFF_V_DIM, ks), BF16),
            pltpu.VMEM((2, ks // tk, tk, tq), F32),
            pltpu.VMEM((2, ks // tk, tk, tq), F32),
            pltpu.VMEM((2, ks, tq), BF16),
            pltpu.VMEM((2, DIFF_V_DIM, tq), F32),
        ],
        compiler_params=_cparams(("arbitrary", "arbitrary")),
        name="diff_attention",
    )(lq, qkv, qkv, qkv, gain_col)


def _dft_kernel(c_ref, s_ref, y_ref, o_ref, acc_scr, *, nk, scale):
    k = pl.program_id(2)
    y = y_ref[...]
    part = (jnp.dot(c_ref[...], y[:, 0:FOURIER_WIDTH], preferred_element_type=F32)
            + jnp.dot(s_ref[...], y[:, FOURIER_WIDTH:2 * FOURIER_WIDTH], preferred_element_type=F32))

    @pl.when(k == 0)
    def _():
        acc_scr[...] = part

    @pl.when(k > 0)
    def _():
        acc_scr[...] += part

    @pl.when(k == nk - 1)
    def _():
        o_ref[...] = (acc_scr[...] * scale).astype(BF16)


def _fourier(yf, cmat, smat, row0, nb, seq):
    tm = min(seq, DFT_ROW_TILE)
    tk = min(seq, DFT_K_TILE)
    ni, nk = seq // tm, seq // tk
    yblk0 = row0 // tk
    kern = functools.partial(_dft_kernel, nk=nk, scale=1.0 / math.sqrt(seq))
    return pl.pallas_call(
        kern,
        grid=(ni, nb, nk),
        in_specs=[
            pl.BlockSpec((tm, tk), lambda i, b, k: (i, k)),
            pl.BlockSpec((tm, tk), lambda i, b, k: (i, k)),
            pl.BlockSpec((tk, 2 * FOURIER_WIDTH), lambda i, b, k: (yblk0 + b * nk + k, 0)),
        ],
        out_specs=pl.BlockSpec((tm, FOURIER_WIDTH), lambda i, b, k: (b * ni + i, 0)),
        out_shape=jax.ShapeDtypeStruct((nb * seq, FOURIER_WIDTH), BF16),
        scratch_shapes=[pltpu.VMEM((tm, FOURIER_WIDTH), F32)],
        compiler_params=_cparams(("arbitrary", "arbitrary", "arbitrary")),
        name="fourier_dft",
    )(cmat, smat, yf)


def _outproj_kernel(x_ref, of_ref, oa_ref, w_ref, g_ref, x1_ref, h_ref):
    y = (jnp.dot(of_ref[...], w_ref[0:FOURIER_WIDTH, :], preferred_element_type=F32)
         + jnp.dot(oa_ref[...], w_ref[FOURIER_WIDTH:D_MODEL, :], preferred_element_type=F32))
    x1 = x_ref[...] + y
    x1_ref[...] = x1
    ms = jnp.mean(x1 * x1, axis=-1, keepdims=True)
    h_ref[...] = ((x1 * lax.rsqrt(ms + EPS)) * g_ref[...]).astype(BF16)


def _outproj(x, o_f, o_a, w_bf, g):
    n = x.shape[0]
    tm = TOKEN_TILE
    return pl.pallas_call(
        _outproj_kernel,
        grid=(n // tm,),
        in_specs=[
            pl.BlockSpec((tm, D_MODEL), lambda i: (i, 0)),
            pl.BlockSpec((tm, FOURIER_WIDTH), lambda i: (i, 0)),
            pl.BlockSpec((tm, ATTN_WIDTH), lambda i: (i, 0)),
            pl.BlockSpec((D_MODEL, D_MODEL), lambda i: (0, 0)),
            pl.BlockSpec((1, D_MODEL), lambda i: (0, 0)),
        ],
        out_specs=[
            pl.BlockSpec((tm, D_MODEL), lambda i: (i, 0)),
            pl.BlockSpec((tm, D_MODEL), lambda i: (i, 0)),
        ],
        out_shape=[
            jax.ShapeDtypeStruct((n, D_MODEL), F32),
            jax.ShapeDtypeStruct((n, D_MODEL), BF16),
        ],
        compiler_params=_cparams(("arbitrary",)),
        name="outproj_norm",
    )(x, o_f, o_a, w_bf, g)


def _ffn_kernel(x1_ref, h_ref, wgu_ref, wd_ref, o_ref, acc_scr):
    hb = h_ref[...]
    fc = FFN_CHUNK
    for j in range(FF_DENSE // fc):
        g = jnp.dot(hb, wgu_ref[:, j * fc:(j + 1) * fc], preferred_element_type=F32)
        u = jnp.dot(hb, wgu_ref[:, FF_DENSE + j * fc:FF_DENSE + (j + 1) * fc], preferred_element_type=F32)
        c = jnp.dot(_silu_mul(g, u).astype(BF16), wd_ref[j * fc:(j + 1) * fc, :], preferred_element_type=F32)
        if j == 0:
            acc_scr[...] = c
        else:
            acc_scr[...] += c
    o_ref[...] = x1_ref[...] + acc_scr[...]


def _dense_ffn(x1, h, wgu_bf, wd_bf):
    n = x1.shape[0]
    tm = TOKEN_TILE
    return pl.pallas_call(
        _ffn_kernel,
        grid=(n // tm,),
        in_specs=[
            pl.BlockSpec((tm, D_MODEL), lambda i: (i, 0)),
            pl.BlockSpec((tm, D_MODEL), lambda i: (i, 0)),
            pl.BlockSpec((D_MODEL, 2 * FF_DENSE), lambda i: (0, 0)),
            pl.BlockSpec((FF_DENSE, D_MODEL), lambda i: (0, 0)),
        ],
        out_specs=pl.BlockSpec((tm, D_MODEL), lambda i: (i, 0)),
        out_shape=jax.ShapeDtypeStruct((n, D_MODEL), F32),
        scratch_shapes=[pltpu.VMEM((tm, D_MODEL), F32)],
        compiler_params=_cparams(("arbitrary",)),
        name="dense_swiglu",
    )(x1, h, wgu_bf, wd_bf)


def _router_kernel(x_ref, g_ref, rw_ref, rt_ref, cnt_ref, carry_scr):
    i = pl.program_id(0)
    tm = TOKEN_TILE

    @pl.when(i == 0)
    def _():
        carry_scr[...] = jnp.zeros_like(carry_scr)

    x = x_ref[...]
    ms = jnp.mean(x * x, axis=-1, keepdims=True)
    h = (x * lax.rsqrt(ms + EPS)) * g_ref[...]
    h_hi = h.astype(BF16)
    h_lo = (h - h_hi.astype(F32)).astype(BF16)
    rw = rw_ref[...]
    rw_hi = rw.astype(BF16)
    rw_lo = (rw - rw_hi.astype(F32)).astype(BF16)
    nt_dims = (((1,), (1,)), ((), ()))
    dg = lambda a, b: lax.dot_general(a, b, nt_dims, preferred_element_type=F32)
    logits = dg(rw_hi, h_hi) + (dg(rw_hi, h_lo) + dg(rw_lo, h_hi))

    eidx = lax.broadcasted_iota(jnp.int32, (N_EXPERTS, tm), 0)
    m1 = jnp.max(logits, axis=0, keepdims=True)
    i1 = jnp.min(jnp.where(logits == m1, eidx, N_EXPERTS), axis=0, keepdims=True)
    oh1 = eidx == i1
    rest = jnp.where(oh1, -jnp.inf, logits)
    m2 = jnp.max(rest, axis=0, keepdims=True)
    i2 = jnp.min(jnp.where(rest == m2, eidx, N_EXPERTS), axis=0, keepdims=True)
    oh2 = eidx == i2
    e = jnp.exp(m2 - m1)
    g1 = 1.0 / (1.0 + e)
    g2 = e / (1.0 + e)

    onehot = jnp.where(oh1 | oh2, 1.0, 0.0)
    upper = (lax.broadcasted_iota(jnp.int32, (tm, tm), 0)
             < lax.broadcasted_iota(jnp.int32, (tm, tm), 1))
    before = jnp.dot(onehot.astype(BF16), jnp.where(upper, 1.0, 0.0).astype(BF16),
                     preferred_element_type=F32)
    rank = before + carry_scr[:, 0:1]
    r1 = jnp.sum(jnp.where(oh1, rank, 0.0), axis=0, keepdims=True)
    r2 = jnp.sum(jnp.where(oh2, rank, 0.0), axis=0, keepdims=True)
    zero = jnp.zeros_like(g1)
    rt_ref[...] = jnp.concatenate(
        [i1.astype(F32), i2.astype(F32), r1, r2, g1, g2, zero, zero], axis=0)
    total = carry_scr[...] + jnp.sum(onehot, axis=1, keepdims=True)
    carry_scr[...] = total
    cnt_ref[...] = total


def _router(x1, g, rw_t):
    n = x1.shape[0]
    tm = TOKEN_TILE
    return pl.pallas_call(
        _router_kernel,
        grid=(n // tm,),
        in_specs=[
            pl.BlockSpec((tm, D_MODEL), lambda i: (i, 0)),
            pl.BlockSpec((1, D_MODEL), lambda i: (0, 0)),
            pl.BlockSpec((N_EXPERTS, D_MODEL), lambda i: (0, 0)),
        ],
        out_specs=[
            pl.BlockSpec((SUBLANES, tm), lambda i: (0, i)),
            pl.BlockSpec((N_EXPERTS, LANES), lambda i: (0, 0)),
        ],
        out_shape=[
            jax.ShapeDtypeStruct((SUBLANES, n), F32),
            jax.ShapeDtypeStruct((N_EXPERTS, LANES), F32),
        ],
        scratch_shapes=[pltpu.VMEM((N_EXPERTS, LANES), F32)],
        compiler_params=_cparams(("arbitrary",)),
        name="router_top2",
    )(x1, g, rw_t)


def _row_copy(src, src_row, dst, dst_row, sem):
    return pltpu.make_async_copy(src.at[pl.ds(src_row, 1)], dst.at[pl.ds(dst_row, 1)], sem)


def _dispatch_kernel(s1_ref, s2_ref, h_ref, xs_ref, rows_scr, sem):
    tm = TOKEN_TILE
    rows_scr[...] = h_ref[...].astype(F32)

    def issue(t, c):
        _row_copy(rows_scr, t, xs_ref, s1_ref[t], sem).start()
        _row_copy(rows_scr, t, xs_ref, s2_ref[t], sem).start()
        return c

    lax.fori_loop(0, tm, issue, 0)
    for _ in range(2):
        pltpu.make_async_copy(rows_scr, xs_ref.at[pl.ds(0, tm)], sem).wait()


def _dispatch(slot1, slot2, h, n_slots):
    n = h.shape[0]
    tm = TOKEN_TILE
    smem = lambda: pl.BlockSpec((tm,), lambda i: (i,), memory_space=pltpu.SMEM)
    return pl.pallas_call(
        _dispatch_kernel,
        grid=(n // tm,),
        in_specs=[smem(), smem(), pl.BlockSpec((tm, D_MODEL), lambda i: (i, 0))],
        out_specs=pl.BlockSpec(memory_space=pl.ANY),
        out_shape=jax.ShapeDtypeStruct((n_slots, D_MODEL), F32),
        scratch_shapes=[pltpu.VMEM((tm, D_MODEL), F32), pltpu.SemaphoreType.DMA(())],
        compiler_params=_cparams(("arbitrary",)),
        name="moe_dispatch",
    )(slot1, slot2, h)


def _expert_kernel(te_ref, nt_ref, tv_ref, xs_ref, wg_ref, wu_ref, wd_ref, ys_ref, xb_scr):
    i = pl.program_id(0)
    j = pl.program_id(1)

    @pl.when(i < nt_ref[0])
    def _():
        @pl.when(j == 0)
        def _():
            rows = lax.broadcasted_iota(jnp.int32, (EXPERT_TILE, 1), 0)
            xb_scr[...] = jnp.where(rows < tv_ref[i], xs_ref[...], 0.0).astype(BF16)

        xb = xb_scr[...]
        g = jnp.dot(xb, wg_ref[0], preferred_element_type=F32)
        u = jnp.dot(xb, wu_ref[0], preferred_element_type=F32)
        c = jnp.dot(_silu_mul(g, u).astype(BF16), wd_ref[0], preferred_element_type=F32)

        @pl.when(j == 0)
        def _():
            ys_ref[...] = c

        @pl.when(j > 0)
        def _():
            ys_ref[...] += c


def _experts(tile_expert, n_tiles, tile_valid, xs, wgu_bf, wd_bf):
    n_slots = xs.shape[0]
    te, fc = EXPERT_TILE, EXPERT_CHUNK
    nj = FF_EXPERT // fc
    max_tiles = n_slots // te

    def tile(i, nt):
        return jnp.minimum(i, nt[0] - 1)

    def chunk(i, j, nt):
        return jnp.where(i < nt[0], j, nj - 1)

    grid_spec = pltpu.PrefetchScalarGridSpec(
        num_scalar_prefetch=3,
        grid=(max_tiles, nj),
        in_specs=[
            pl.BlockSpec((te, D_MODEL), lambda i, j, e, nt, tv: (tile(i, nt), 0)),
            pl.BlockSpec((1, D_MODEL, fc), lambda i, j, e, nt, tv: (e[tile(i, nt)], 0, chunk(i, j, nt))),
            pl.BlockSpec((1, D_MODEL, fc), lambda i, j, e, nt, tv: (e[tile(i, nt)], 0, nj + chunk(i, j, nt))),
            pl.BlockSpec((1, fc, D_MODEL), lambda i, j, e, nt, tv: (e[tile(i, nt)], chunk(i, j, nt), 0)),
        ],
        out_specs=pl.BlockSpec((te, D_MODEL), lambda i, j, e, nt, tv: (tile(i, nt), 0)),
        scratch_shapes=[pltpu.VMEM((te, D_MODEL), BF16)],
    )
    return pl.pallas_call(
        _expert_kernel,
        grid_spec=grid_spec,
        out_shape=jax.ShapeDtypeStruct((n_slots, D_MODEL), F32),
        compiler_params=_cparams(("arbitrary", "arbitrary")),
        name="expert_swiglu",
    )(tile_expert, n_tiles, tile_valid, xs, wgu_bf, wgu_bf, wd_bf)


def _combine_kernel(s1_ref, s2_ref, rt_ref, x1_ref, g_ref, ys_ref, o_ref, a_scr, b_scr, sem):
    tm = TOKEN_TILE

    def issue(t, c):
        _row_copy(ys_ref, s1_ref[t], a_scr, t, sem).start()
        _row_copy(ys_ref, s2_ref[t], b_scr, t, sem).start()
        return c

    lax.fori_loop(0, tm, issue, 0)

    rt = jnp.concatenate([rt_ref[...], jnp.zeros((LANES - SUBLANES, tm), F32)], axis=0)
    rt_t = rt.T
    g1 = rt_t[:, 4:5]
    g2 = rt_t[:, 5:6]

    pltpu.make_async_copy(ys_ref.at[pl.ds(0, tm)], a_scr, sem).wait()
    pltpu.make_async_copy(ys_ref.at[pl.ds(0, tm)], b_scr, sem).wait()

    x2 = x1_ref[...] + (g1 * a_scr[...] + g2 * b_scr[...])
    ms = jnp.mean(x2 * x2, axis=-1, keepdims=True)
    o_ref[...] = (x2 * lax.rsqrt(ms + EPS)) * g_ref[...]


def _combine(slot1, slot2, rt, x1, g_final, ys):
    n = x1.shape[0]
    tm = TOKEN_TILE
    smem = lambda: pl.BlockSpec((tm,), lambda i: (i,), memory_space=pltpu.SMEM)
    return pl.pallas_call(
        _combine_kernel,
        grid=(n // tm,),
        in_specs=[
            smem(), smem(),
            pl.BlockSpec((SUBLANES, tm), lambda i: (0, i)),
            pl.BlockSpec((tm, D_MODEL), lambda i: (i, 0)),
            pl.BlockSpec((1, D_MODEL), lambda i: (0, 0)),
            pl.BlockSpec(memory_space=pl.ANY),
        ],
        out_specs=pl.BlockSpec((tm, D_MODEL), lambda i: (i, 0)),
        out_shape=jax.ShapeDtypeStruct((n, D_MODEL), F32),
        scratch_shapes=[
            pltpu.VMEM((tm, D_MODEL), F32),
            pltpu.VMEM((tm, D_MODEL), F32),
            pltpu.SemaphoreType.DMA(()),
        ],
        compiler_params=_cparams(("arbitrary",)),
        name="moe_combine_norm",
    )(slot1, slot2, rt, x1, g_final, ys)


def _rope_tables(seq_max):
    half = ROT_DIM // 2
    inv = ROPE_THETA ** (-jnp.arange(0, ROT_DIM, 2, dtype=F32) / ROT_DIM)
    ang = jnp.arange(seq_max, dtype=F32)[:, None] * inv[None, :]
    cos, sin = jnp.cos(ang), jnp.sin(ang)
    ones = jnp.ones((seq_max, DIFF_HEAD_DIM - ROT_DIM), F32)
    zeros = jnp.zeros((seq_max, DIFF_HEAD_DIM - ROT_DIM), F32)
    z8 = jnp.zeros((seq_max, half), F32)
    ct = jnp.concatenate([cos, cos, ones], axis=1)
    sa = jnp.concatenate([-sin, z8, zeros], axis=1)
    sb = jnp.concatenate([z8, sin, zeros], axis=1)
    reps = 2 * LANES // DIFF_HEAD_DIM
    return tuple(jnp.tile(t, (1, reps)) for t in (ct, sa, sb))


def _channel_dft_matrix():
    c = np.arange(FOURIER_GROUP_DIM)
    ang = 2.0 * np.pi * ((c[:, None] * c[None, :]) % FOURIER_GROUP_DIM) / FOURIER_GROUP_DIM
    scale = FOURIER_GROUP_DIM ** -0.5
    eye = np.eye(FOURIER_GROUPS)
    bd = np.concatenate([np.kron(eye, np.cos(ang) * scale), np.kron(eye, np.sin(ang) * scale)], axis=1)
    return jnp.asarray(bd, dtype=BF16)


def _position_dft_matrices(seq):
    lo = FOURIER_GROUP_DIM
    hi = seq // lo
    sp = jnp.arange(seq, dtype=jnp.int32)[None, :]
    a = jnp.arange(hi, dtype=jnp.int32)[:, None]
    b = jnp.arange(lo, dtype=jnp.int32)[:, None]
    w = 2.0 * math.pi / seq
    ang_a = ((a * lo * sp) % seq).astype(F32) * w
    ang_b = ((b * sp) % seq).astype(F32) * w
    ca, sa = jnp.cos(ang_a)[:, None, :], jnp.sin(ang_a)[:, None, :]
    cb, sb = jnp.cos(ang_b)[None, :, :], jnp.sin(ang_b)[None, :, :]
    cmat = (ca * cb - sa * sb).reshape(seq, seq).astype(BF16)
    smat = (-(sa * cb + ca * sb)).reshape(seq, seq).astype(BF16)
    return cmat, smat


def kernel(x_prompt, x_sample, norm_mix, w_in, lambda_qk, subln_gain, w_out, norm_ffn, ffn_w_gate_up,
           ffn_w_down, router_w, expert_w_gate_up, expert_w_down, final_norm):
    nb_p, seq_p, _ = x_prompt.shape
    nb_s, seq_s, _ = x_sample.shape
    n_p, n_s = nb_p * seq_p, nb_s * seq_s
    n = n_p + n_s
    assert seq_p % TOKEN_TILE == 0 and seq_s % TOKEN_TILE == 0
    assert n_p % seq_s == 0 and seq_p % ATTN_K_TILE == 0 and seq_s % ATTN_K_TILE == 0
    assert n_p % min(seq_s, DFT_K_TILE) == 0

    x = jnp.concatenate([x_prompt.reshape(n_p, D_MODEL), x_sample.reshape(n_s, D_MODEL)], axis=0)

    ct, sa, sb = _rope_tables(max(seq_p, seq_s))
    bd = _channel_dft_matrix()
    dft_p = _position_dft_matrices(seq_p)
    dft_s = dft_p if seq_s == seq_p else _position_dft_matrices(seq_s)

    rt = slot1 = slot2 = None
    for layer in range(DEPTH):
        lam_init = 0.8 - 0.6 * math.exp(-0.3 * layer)
        yf, qkv = _norm_inproj(x, norm_mix[layer][None, :], w_in[layer].astype(BF16), bd, ct, sa, sb,
                               seq_p, n_p, seq_s)
        gain_col = subln_gain[layer][:, None]
        o_a = jnp.concatenate([
            _attention(qkv, lambda_qk[layer], gain_col, 0, nb_p, seq_p, lam_init),
            _attention(qkv, lambda_qk[layer], gain_col, n_p, nb_s, seq_s, lam_init)], axis=0)
        o_f = jnp.concatenate([
            _fourier(yf, dft_p[0], dft_p[1], 0, nb_p, seq_p),
            _fourier(yf, dft_s[0], dft_s[1], n_p, nb_s, seq_s)], axis=0)
        x1, h = _outproj(x, o_f, o_a, w_out[layer].astype(BF16), norm_ffn[layer][None, :])
        if layer % 2 == 0:
            x = _dense_ffn(x1, h, ffn_w_gate_up[layer // 2].astype(BF16), ffn_w_down[layer // 2].astype(BF16))
        else:
            j = layer // 2
            rt, cnt = _router(x1, norm_ffn[layer][None, :], router_w[j].T)
            te = EXPERT_TILE
            counts = cnt[:, 0].astype(jnp.int32)
            padded = ((counts + te - 1) // te) * te
            ends = jnp.cumsum(padded)
            offs = ends - padded
            e1, e2 = rt[0].astype(jnp.int32), rt[1].astype(jnp.int32)
            slot1 = offs[e1] + rt[2].astype(jnp.int32)
            slot2 = offs[e2] + rt[3].astype(jnp.int32)
            n_slots = 2 * n + N_EXPERTS * te
            starts = jnp.arange(n_slots // te, dtype=jnp.int32) * te
            tile_expert = jnp.minimum(jnp.sum((starts[:, None] >= ends[None, :]).astype(jnp.int32), axis=1),
                                      N_EXPERTS - 1)
            tile_valid = jnp.clip(counts[tile_expert] - (starts - offs[tile_expert]), 0, te).astype(jnp.int32)
            n_tiles = (ends[-1:] // te).astype(jnp.int32)
            xs = _dispatch(slot1, slot2, h, n_slots)
            ys = _experts(tile_expert, n_tiles, tile_valid, xs,
                          expert_w_gate_up[j].astype(BF16), expert_w_down[j].astype(BF16))
            x = _combine(slot1, slot2, rt, x1, final_norm[None, :], ys)
    assert DEPTH % 2 == 0
    y_p = x[:n_p].reshape(nb_p, seq_p, D_MODEL)
    y_s = x[n_p:].reshape(nb_s, seq_s, D_MODEL)
    return (y_p, y_s)
```

```python
import functools
import math

import numpy as np
import jax
import jax.numpy as jnp
from jax import lax
from jax.experimental import pallas as pl
from jax.experimental.pallas import tpu as pltpu

F32 = jnp.float32
BF16 = jnp.bfloat16

D_MODEL = 1024
DEPTH = 2
FOURIER_WIDTH = 256
FOURIER_GROUPS = 4
FOURIER_GROUP_DIM = 64
ATTN_WIDTH = 768
DIFF_HEAD_DIM = 64
DIFF_V_DIM = 128
N_DIFF_HEADS = 6
QK_WIDTH = 768
IN_PROJ_WIDTH = 2560
ROT_DIM = 16
ROPE_THETA = 500000.0
FF_DENSE = 2816
N_EXPERTS = 8
FF_EXPERT = 3584
EPS = 1e-5

LANES = 128
SUBLANES = 8
VMEM_LIMIT_BYTES = 56 * 1024 * 1024

TOKEN_TILE = 512
ATTN_Q_TILE = 512
ATTN_K_TILE = 512
ATTN_K_SUPER = 2048
LOG2E = 1.4426950408889634
DFT_ROW_TILE = 1024
DFT_K_TILE = 2048
FFN_CHUNK = 256
EXPERT_TILE = 1024
EXPERT_CHUNK = 1792


def _cparams(sem):
    return pltpu.CompilerParams(dimension_semantics=sem, vmem_limit_bytes=VMEM_LIMIT_BYTES)


def _aligned(start, multiple):
    return start if isinstance(start, int) else pl.multiple_of(start, multiple)


def _silu_mul(g, u):
    return (g * (1.0 / (1.0 + jnp.exp(-g)))) * u


class _Split:
    def __init__(self, tiles_a, b_blk0):
        self.tiles_a, self.b_blk0 = tiles_a, b_blk0

    def specs(self, tm, width):
        ta, b0 = self.tiles_a, self.b_blk0
        return [pl.BlockSpec((tm, width), lambda i: (jnp.minimum(i, ta - 1), 0)),
                pl.BlockSpec((tm, width), lambda i: (b0 + jnp.maximum(i - ta, 0), 0))]

    def pick(self, a_ref, b_ref):
        return jnp.where(pl.program_id(0) < self.tiles_a, a_ref[...], b_ref[...])


def _norm_inproj_kernel(xa_ref, xb_ref, g_ref, w_ref, bd_ref, ct_ref, sa_ref, sb_ref, yf_ref, qkv_ref, h_scr,
                        *, split):
    x = split.pick(xa_ref, xb_ref)
    ms = jnp.mean(x * x, axis=-1, keepdims=True)
    h_scr[...] = ((x * lax.rsqrt(ms + EPS)) * g_ref[...]).astype(BF16)
    hb = h_scr[...]
    uf = jnp.dot(hb, w_ref[:, 0:FOURIER_WIDTH], preferred_element_type=F32)
    yf_ref[...] = jnp.dot(uf.astype(BF16), bd_ref[...], preferred_element_type=F32).astype(BF16)
    ct, sa, sb = ct_ref[...], sa_ref[...], sb_ref[...]
    width = 2 * LANES
    for blk in range(2 * QK_WIDTH // width):
        c0 = FOURIER_WIDTH + blk * width
        t = jnp.dot(hb, w_ref[:, c0:c0 + width], preferred_element_type=F32)
        r = t * ct + pltpu.roll(t, width - ROT_DIM // 2, 1) * sa + pltpu.roll(t, ROT_DIM // 2, 1) * sb
        if blk < QK_WIDTH // width:
            r = r * (DIFF_HEAD_DIM ** -0.5 * LOG2E)
        qkv_ref[:, blk * width:(blk + 1) * width] = r.astype(BF16)
    v0 = FOURIER_WIDTH + 2 * QK_WIDTH
    for blk in range(ATTN_WIDTH // width):
        t = jnp.dot(hb, w_ref[:, v0 + blk * width:v0 + (blk + 1) * width], preferred_element_type=F32)
        qkv_ref[:, 2 * QK_WIDTH + blk * width:2 * QK_WIDTH + (blk + 1) * width] = t.astype(BF16)


def _norm_inproj(xa, xb, split, n, g, w_bf, bd, ct, sa, sb, seq_p, seq_s):
    tm = TOKEN_TILE
    tiles_p = split.tiles_a
    per_p, per_s = seq_p // tm, seq_s // tm

    def pos_map(i):
        return (jnp.where(i < tiles_p, i % per_p, (i - tiles_p) % per_s), 0)

    tab = pl.BlockSpec((tm, 2 * LANES), pos_map)
    return pl.pallas_call(
        functools.partial(_norm_inproj_kernel, split=split),
        grid=(n // tm,),
        in_specs=split.specs(tm, D_MODEL) + [
            pl.BlockSpec((1, D_MODEL), lambda i: (0, 0)),
            pl.BlockSpec((D_MODEL, IN_PROJ_WIDTH), lambda i: (0, 0)),
            pl.BlockSpec((FOURIER_WIDTH, 2 * FOURIER_WIDTH), lambda i: (0, 0)),
            tab, tab, tab,
        ],
        out_specs=[
            pl.BlockSpec((tm, 2 * FOURIER_WIDTH), lambda i: (i, 0)),
            pl.BlockSpec((tm, 2 * QK_WIDTH + ATTN_WIDTH), lambda i: (i, 0)),
        ],
        out_shape=[
            jax.ShapeDtypeStruct((n, 2 * FOURIER_WIDTH), BF16),
            jax.ShapeDtypeStruct((n, 2 * QK_WIDTH + ATTN_WIDTH), BF16),
        ],
        scratch_shapes=[pltpu.VMEM((tm, D_MODEL), BF16)],
        compiler_params=_cparams(("arbitrary",)),
        name="norm_inproj",
    )(xa, xb, g, w_bf, bd, ct, sa, sb)


def _attn_kernel(lq_ref, q_ref, k_ref, v_ref, g_ref, o_ref, vt_scr, sa_scr, sb_scr, p_scr, acc_scr, *,
                 seq, lam_init):
    tq, tk = ATTN_Q_TILE, ATTN_K_TILE
    ks = min(seq, ATTN_K_SUPER)
    nsb, nkc = seq // ks, ks // tk
    groups = tk // SUBLANES
    n_units = (seq // tq) * nsb
    assert n_units % 2 == 0
    s_bufs = (sa_scr, sb_scr)
    for sb in range(nsb):
        for j in range(nkc):
            r0 = sb * ks + j * tk
            vt_scr[sb, :, j * tk:(j + 1) * tk] = v_ref[r0:r0 + tk, :].astype(F32).T.astype(BF16)

    lq = lq_ref[...]
    lam = (jnp.exp(jnp.sum(lq[0:1, :] * lq[1:2, :], axis=1, keepdims=True))
           - jnp.exp(jnp.sum(lq[2:3, :] * lq[3:4, :], axis=1, keepdims=True)) + lam_init)
    gain = g_ref[...]
    first = lax.broadcasted_iota(jnp.int32, (1, DIFF_V_DIM), 1) < DIFF_HEAD_DIM
    nt_dims = (((1,), (1,)), ((), ()))
    comps = (0, 1)

    def load_q(u):
        q = q_ref[pl.ds(_aligned((u // nsb) * tq, tq), tq), :]
        return (jnp.where(first, q, jnp.zeros_like(q)), jnp.where(first, jnp.zeros_like(q), q))

    def score_chunk(u, qc, s_scr, j, mx):
        kblk = k_ref[pl.ds(_aligned((u % nsb) * ks + j * tk, tk), tk), :]
        out = []
        for c in comps:
            s = lax.dot_general(kblk, qc[c], nt_dims, preferred_element_type=F32)
            s_scr[c, j] = s
            out.append(jnp.maximum(mx[c], jnp.max(s.reshape(groups, SUBLANES, tq), axis=0)))
        return out

    def exp_chunk(s_scr, j, m_new, ls):
        out = []
        for c in comps:
            p = jnp.exp2(s_scr[c, j] - m_new[c])
            out.append(ls[c] + jnp.sum(p.reshape(groups, SUBLANES, tq), axis=0))
            p_scr[c, j * tk:(j + 1) * tk, :] = p.astype(BF16)
        return out

    def unit(u, slot, st, with_next):
        mx, m, l = st[0:2], st[2:4], st[4:6]
        sb, qi = u % nsb, u // nsb
        m_new = [jnp.maximum(m[c], jnp.max(mx[c], axis=0, keepdims=True)) for c in comps]
        alpha = [jnp.exp2(m[c] - m_new[c]) for c in comps]
        ls = [jnp.zeros((SUBLANES, tq), F32) for _ in comps]
        mx_next = [jnp.full((SUBLANES, tq), -jnp.inf, F32) for _ in comps]
        qc_next = load_q(u + 1) if with_next else None
        for j in range(nkc):
            if with_next:
                mx_next = score_chunk(u + 1, qc_next, s_bufs[1 - slot], j, mx_next)
            ls = exp_chunk(s_bufs[slot], j, m_new, ls)
        l_new = [alpha[c] * l[c] + jnp.sum(ls[c], axis=0, keepdims=True) for c in comps]
        vt = vt_scr[sb]
        for c in comps:
            acc_scr[c] = alpha[c] * acc_scr[c] + jnp.dot(vt, p_scr[c], preferred_element_type=F32)

        last = jnp.asarray(sb == nsb - 1)

        @pl.when(last)
        def _():
            o_t = acc_scr[0] * (1.0 / l_new[0]) - lam * (acc_scr[1] * (1.0 / l_new[1]))
            ms = jnp.mean(o_t * o_t, axis=0, keepdims=True)
            y_t = ((o_t * lax.rsqrt(ms + EPS)) * gain) * (1.0 - lam_init)
            o_ref[pl.ds(_aligned(qi * tq, tq), tq), :] = y_t.T.astype(BF16)
            acc_scr[...] = jnp.zeros_like(acc_scr)

        m_out = [jnp.where(last, -jnp.inf, m_new[c]) for c in comps]
        l_out = [jnp.where(last, 0.0, l_new[c]) for c in comps]
        return (mx_next[0], mx_next[1], m_out[0], m_out[1], l_out[0], l_out[1])

    acc_scr[...] = jnp.zeros_like(acc_scr)
    mx0 = [jnp.full((SUBLANES, tq), -jnp.inf, F32) for _ in comps]
    qc0 = load_q(0)
    for j in range(nkc):
        mx0 = score_chunk(0, qc0, s_bufs[0], j, mx0)
    neg = jnp.full((1, tq), -jnp.inf, F32)
    zero = jnp.zeros((1, tq), F32)
    st = (mx0[0], mx0[1], neg, neg, zero, zero)

    def pair(i, st):
        st = unit(2 * i, 0, st, True)
        return unit(2 * i + 1, 1, st, True)

    st = lax.fori_loop(0, n_units // 2 - 1, pair, st)
    st = unit(n_units - 2, 0, st, True)
    unit(n_units - 1, 1, st, False)


def _attention(qkv, lq, gain_col, row0, nb, seq, lam_init):
    blk0 = row0 // seq
    nh = N_DIFF_HEADS
    tq, tk = ATTN_Q_TILE, ATTN_K_TILE
    ks = min(seq, ATTN_K_SUPER)
    kern = functools.partial(_attn_kernel, seq=seq, lam_init=lam_init)
    return pl.pallas_call(
        kern,
        grid=(nb, nh),
        in_specs=[
            pl.BlockSpec((4, DIFF_HEAD_DIM), lambda b, h: (0, 0)),
            pl.BlockSpec((seq, LANES), lambda b, h: (blk0 + b, h)),
            pl.BlockSpec((seq, LANES), lambda b, h: (blk0 + b, nh + h)),
            pl.BlockSpec((seq, LANES), lambda b, h: (blk0 + b, 2 * nh + h)),
            pl.BlockSpec((DIFF_V_DIM, 1), lambda b, h: (0, 0)),
        ],
        out_specs=pl.BlockSpec((seq, LANES), lambda b, h: (b, h)),
        out_shape=jax.ShapeDtypeStruct((nb * seq, ATTN_WIDTH), BF16),
        scratch_shapes=[
            pltpu.VMEM((seq // ks, DIFF_V_DIM, ks), BF16),
            pltpu.VMEM((2, ks // tk, tk, tq), F32),
            pltpu.VMEM((2, ks // tk, tk, tq), F32),
            pltpu.VMEM((2, ks, tq), BF16),
            pltpu.VMEM((2, DIFF_V_DIM, tq), F32),
        ],
        compiler_params=_cparams(("arbitrary", "arbitrary")),
        name="diff_attention",
    )(lq, qkv, qkv, qkv, gain_col)


def _dft_kernel(c_ref, s_ref, y_ref, o_ref, acc_scr, *, nk, scale):
    k = pl.program_id(2)
    y = y_ref[...]
    part = (jnp.dot(c_ref[...], y[:, 0:FOURIER_WIDTH], preferred_element_type=F32)
            + jnp.dot(s_ref[...], y[:, FOURIER_WIDTH:2 * FOURIER_WIDTH], preferred_element_type=F32))

    @pl.when(k == 0)
    def _():
        acc_scr[...] = part

    @pl.when(k > 0)
    def _():
        acc_scr[...] += part

    @pl.when(k == nk - 1)
    def _():
        o_ref[...] = (acc_scr[...] * scale).astype(BF16)


def _fourier(yf, cmat, smat, row0, nb, seq):
    tm = min(seq, DFT_ROW_TILE)
    tk = min(seq, DFT_K_TILE)
    ni, nk = seq // tm, seq // tk
    yblk0 = row0 // tk
    kern = functools.partial(_dft_kernel, nk=nk, scale=1.0 / math.sqrt(seq))
    return pl.pallas_call(
        kern,
        grid=(ni, nb, nk),
        in_specs=[
            pl.BlockSpec((tm, tk), lambda i, b, k: (i, k)),
            pl.BlockSpec((tm, tk), lambda i, b, k: (i, k)),
            pl.BlockSpec((tk, 2 * FOURIER_WIDTH), lambda i, b, k: (yblk0 + b * nk + k, 0)),
        ],
        out_specs=pl.BlockSpec((tm, FOURIER_WIDTH), lambda i, b, k: (b * ni + i, 0)),
        out_shape=jax.ShapeDtypeStruct((nb * seq, FOURIER_WIDTH), BF16),
        scratch_shapes=[pltpu.VMEM((tm, FOURIER_WIDTH), F32)],
        compiler_params=_cparams(("arbitrary", "arbitrary", "arbitrary")),
        name="fourier_dft",
    )(cmat, smat, yf)


def _outproj_kernel(xa_ref, xb_ref, ofa_ref, ofb_ref, oaa_ref, oab_ref, w_ref, g_ref, x1_ref, h_ref, *,
                    x_split, o_split):
    y = (jnp.dot(o_split.pick(ofa_ref, ofb_ref), w_ref[0:FOURIER_WIDTH, :], preferred_element_type=F32)
         + jnp.dot(o_split.pick(oaa_ref, oab_ref), w_ref[FOURIER_WIDTH:D_MODEL, :], preferred_element_type=F32))
    x1 = x_split.pick(xa_ref, xb_ref) + y
    x1_ref[...] = x1
    ms = jnp.mean(x1 * x1, axis=-1, keepdims=True)
    h_ref[...] = ((x1 * lax.rsqrt(ms + EPS)) * g_ref[...]).astype(BF16)


def _outproj(xa, xb, x_split, o_f, o_a, o_split, n, w_bf, g):
    tm = TOKEN_TILE
    return pl.pallas_call(
        functools.partial(_outproj_kernel, x_split=x_split, o_split=o_split),
        grid=(n // tm,),
        in_specs=(x_split.specs(tm, D_MODEL) + o_split.specs(tm, FOURIER_WIDTH) + o_split.specs(tm, ATTN_WIDTH) + [
            pl.BlockSpec((D_MODEL, D_MODEL), lambda i: (0, 0)),
            pl.BlockSpec((1, D_MODEL), lambda i: (0, 0)),
        ]),
        out_specs=[
            pl.BlockSpec((tm, D_MODEL), lambda i: (i, 0)),
            pl.BlockSpec((tm, D_MODEL), lambda i: (i, 0)),
        ],
        out_shape=[
            jax.ShapeDtypeStruct((n, D_MODEL), F32),
            jax.ShapeDtypeStruct((n, D_MODEL), BF16),
        ],
        compiler_params=_cparams(("arbitrary",)),
        name="outproj_norm",
    )(xa, xb, o_f[0], o_f[1], o_a[0], o_a[1], w_bf, g)


def _ffn_kernel(x1_ref, h_ref, wgu_ref, wd_ref, o_ref, acc_scr):
    hb = h_ref[...]
    fc = FFN_CHUNK
    for j in range(FF_DENSE // fc):
        g = jnp.dot(hb, wgu_ref[:, j * fc:(j + 1) * fc], preferred_element_type=F32)
        u = jnp.dot(hb, wgu_ref[:, FF_DENSE + j * fc:FF_DENSE + (j + 1) * fc], preferred_element_type=F32)
        c = jnp.dot(_silu_mul(g, u).astype(BF16), wd_ref[j * fc:(j + 1) * fc, :], preferred_element_type=F32)
        if j == 0:
            acc_scr[...] = c
        else:
            acc_scr[...] += c
    o_ref[...] = x1_ref[...] + acc_scr[...]


def _dense_ffn(x1, h, wgu_bf, wd_bf):
    n = x1.shape[0]
    tm = TOKEN_TILE
    return pl.pallas_call(
        _ffn_kernel,
        grid=(n // tm,),
        in_specs=[
            pl.BlockSpec((tm, D_MODEL), lambda i: (i, 0)),
            pl.BlockSpec((tm, D_MODEL), lambda i: (i, 0)),
            pl.BlockSpec((D_MODEL, 2 * FF_DENSE), lambda i: (0, 0)),
            pl.BlockSpec((FF_DENSE, D_MODEL), lambda i: (0, 0)),
        ],
        out_specs=pl.BlockSpec((tm, D_MODEL), lambda i: (i, 0)),
        out_shape=jax.ShapeDtypeStruct((n, D_MODEL), F32),
        scratch_shapes=[pltpu.VMEM((tm, D_MODEL), F32)],
        compiler_params=_cparams(("arbitrary",)),
        name="dense_swiglu",
    )(x1, h, wgu_bf, wd_bf)


def _router_kernel(x_ref, g_ref, rw_ref, rt_ref, cnt_ref, carry_scr):
    i = pl.program_id(0)
    tm = TOKEN_TILE

    @pl.when(i == 0)
    def _():
        carry_scr[...] = jnp.zeros_like(carry_scr)

    x = x_ref[...]
    ms = jnp.mean(x * x, axis=-1, keepdims=True)
    h = (x * lax.rsqrt(ms + EPS)) * g_ref[...]
    h_hi = h.astype(BF16)
    h_lo = (h - h_hi.astype(F32)).astype(BF16)
    rw = rw_ref[...]
    rw_hi = rw.astype(BF16)
    rw_lo = (rw - rw_hi.astype(F32)).astype(BF16)
    nt_dims = (((1,), (1,)), ((), ()))
    dg = lambda a, b: lax.dot_general(a, b, nt_dims, preferred_element_type=F32)
    logits = dg(rw_hi, h_hi) + (dg(rw_hi, h_lo) + dg(rw_lo, h_hi))

    eidx = lax.broadcasted_iota(jnp.int32, (N_EXPERTS, tm), 0)
    m1 = jnp.max(logits, axis=0, keepdims=True)
    i1 = jnp.min(jnp.where(logits == m1, eidx, N_EXPERTS), axis=0, keepdims=True)
    oh1 = eidx == i1
    rest = jnp.where(oh1, -jnp.inf, logits)
    m2 = jnp.max(rest, axis=0, keepdims=True)
    i2 = jnp.min(jnp.where(rest == m2, eidx, N_EXPERTS), axis=0, keepdims=True)
    oh2 = eidx == i2
    e = jnp.exp(m2 - m1)
    g1 = 1.0 / (1.0 + e)
    g2 = e / (1.0 + e)

    onehot = jnp.where(oh1 | oh2, 1.0, 0.0)
    upper = (lax.broadcasted_iota(jnp.int32, (tm, tm), 0)
             < lax.broadcasted_iota(jnp.int32, (tm, tm), 1))
    before = jnp.dot(onehot.astype(BF16), jnp.where(upper, 1.0, 0.0).astype(BF16),
                     preferred_element_type=F32)
    rank = before + carry_scr[:, 0:1]
    r1 = jnp.sum(jnp.where(oh1, rank, 0.0), axis=0, keepdims=True)
    r2 = jnp.sum(jnp.where(oh2, rank, 0.0), axis=0, keepdims=True)
    zero = jnp.zeros_like(g1)
    rt_ref[...] = jnp.concatenate(
        [i1.astype(F32), i2.astype(F32), r1, r2, g1, g2, zero, zero], axis=0)
    total = carry_scr[...] + jnp.sum(onehot, axis=1, keepdims=True)
    carry_scr[...] = total
    cnt_ref[...] = total


def _router(x1, g, rw_t):
    n = x1.shape[0]
    tm = TOKEN_TILE
    return pl.pallas_call(
        _router_kernel,
        grid=(n // tm,),
        in_specs=[
            pl.BlockSpec((tm, D_MODEL), lambda i: (i, 0)),
            pl.BlockSpec((1, D_MODEL), lambda i: (0, 0)),
            pl.BlockSpec((N_EXPERTS, D_MODEL), lambda i: (0, 0)),
        ],
        out_specs=[
            pl.BlockSpec((SUBLANES, tm), lambda i: (0, i)),
            pl.BlockSpec((N_EXPERTS, LANES), lambda i: (0, 0)),
        ],
        out_shape=[
            jax.ShapeDtypeStruct((SUBLANES, n), F32),
            jax.ShapeDtypeStruct((N_EXPERTS, LANES), F32),
        ],
        scratch_shapes=[pltpu.VMEM((N_EXPERTS, LANES), F32)],
        compiler_params=_cparams(("arbitrary",)),
        name="router_top2",
    )(x1, g, rw_t)


def _row_copy(src, src_row, dst, dst_row, sem):
    return pltpu.make_async_copy(src.at[pl.ds(src_row, 1)], dst.at[pl.ds(dst_row, 1)], sem)


def _dispatch_kernel(s1_ref, s2_ref, h_ref, xs_ref, rows_scr, sem):
    tm = TOKEN_TILE
    rows_scr[...] = h_ref[...].astype(F32)

    def issue(t, c):
        _row_copy(rows_scr, t, xs_ref, s1_ref[t], sem).start()
        _row_copy(rows_scr, t, xs_ref, s2_ref[t], sem).start()
        return c

    lax.fori_loop(0, tm, issue, 0)
    for _ in range(2):
        pltpu.make_async_copy(rows_scr, xs_ref.at[pl.ds(0, tm)], sem).wait()


def _dispatch(slot1, slot2, h, n_slots):
    n = h.shape[0]
    tm = TOKEN_TILE
    smem = lambda: pl.BlockSpec((tm,), lambda i: (i,), memory_space=pltpu.SMEM)
    return pl.pallas_call(
        _dispatch_kernel,
        grid=(n // tm,),
        in_specs=[smem(), smem(), pl.BlockSpec((tm, D_MODEL), lambda i: (i, 0))],
        out_specs=pl.BlockSpec(memory_space=pl.ANY),
        out_shape=jax.ShapeDtypeStruct((n_slots, D_MODEL), F32),
        scratch_shapes=[pltpu.VMEM((tm, D_MODEL), F32), pltpu.SemaphoreType.DMA(())],
        compiler_params=_cparams(("arbitrary",)),
        name="moe_dispatch",
    )(slot1, slot2, h)


def _expert_kernel(te_ref, nt_ref, tv_ref, xs_ref, wg_ref, wu_ref, wd_ref, ys_ref, xb_scr):
    i = pl.program_id(0)
    j = pl.program_id(1)

    @pl.when(i < nt_ref[0])
    def _():
        @pl.when(j == 0)
        def _():
            rows = lax.broadcasted_iota(jnp.int32, (EXPERT_TILE, 1), 0)
            xb_scr[...] = jnp.where(rows < tv_ref[i], xs_ref[...], 0.0).astype(BF16)
            ys_ref[...] = jnp.zeros_like(ys_ref)

        xb = xb_scr[...]
        fc = FFN_CHUNK
        for t in range(EXPERT_CHUNK // fc):
            g = jnp.dot(xb, wg_ref[0, :, t * fc:(t + 1) * fc], preferred_element_type=F32)
            u = jnp.dot(xb, wu_ref[0, :, t * fc:(t + 1) * fc], preferred_element_type=F32)
            ys_ref[...] += jnp.dot(_silu_mul(g, u).astype(BF16), wd_ref[0, t * fc:(t + 1) * fc, :],
                                   preferred_element_type=F32)


def _experts(tile_expert, n_tiles, tile_valid, xs, wgu_bf, wd_bf):
    n_slots = xs.shape[0]
    te, fc = EXPERT_TILE, EXPERT_CHUNK
    nj = FF_EXPERT // fc
    max_tiles = n_slots // te

    def tile(i, nt):
        return jnp.minimum(i, nt[0] - 1)

    def chunk(i, j, nt):
        return jnp.where(i < nt[0], j, nj - 1)

    grid_spec = pltpu.PrefetchScalarGridSpec(
        num_scalar_prefetch=3,
        grid=(max_tiles, nj),
        in_specs=[
            pl.BlockSpec((te, D_MODEL), lambda i, j, e, nt, tv: (tile(i, nt), 0)),
            pl.BlockSpec((1, D_MODEL, fc), lambda i, j, e, nt, tv: (e[tile(i, nt)], 0, chunk(i, j, nt))),
            pl.BlockSpec((1, D_MODEL, fc), lambda i, j, e, nt, tv: (e[tile(i, nt)], 0, nj + chunk(i, j, nt))),
            pl.BlockSpec((1, fc, D_MODEL), lambda i, j, e, nt, tv: (e[tile(i, nt)], chunk(i, j, nt), 0)),
        ],
        out_specs=pl.BlockSpec((te, D_MODEL), lambda i, j, e, nt, tv: (tile(i, nt), 0)),
        scratch_shapes=[pltpu.VMEM((te, D_MODEL), BF16)],
    )
    return pl.pallas_call(
        _expert_kernel,
        grid_spec=grid_spec,
        out_shape=jax.ShapeDtypeStruct((n_slots, D_MODEL), F32),
        compiler_params=_cparams(("arbitrary", "arbitrary")),
        name="expert_swiglu",
    )(tile_expert, n_tiles, tile_valid, xs, wgu_bf, wgu_bf, wd_bf)


def _combine_kernel(s1_ref, s2_ref, rt_ref, x1_ref, g_ref, ys_ref, op_ref, os_ref, a_scr, b_scr, sem, *,
                    tiles_p):
    tm = TOKEN_TILE
    i = pl.program_id(0)

    def issue(t, c):
        _row_copy(ys_ref, s1_ref[t], a_scr, t, sem).start()
        _row_copy(ys_ref, s2_ref[t], b_scr, t, sem).start()
        return c

    lax.fori_loop(0, tm, issue, 0)

    rt = jnp.concatenate([rt_ref[...], jnp.zeros((LANES - SUBLANES, tm), F32)], axis=0)
    rt_t = rt.T
    g1 = rt_t[:, 4:5]
    g2 = rt_t[:, 5:6]

    pltpu.make_async_copy(ys_ref.at[pl.ds(0, tm)], a_scr, sem).wait()
    pltpu.make_async_copy(ys_ref.at[pl.ds(0, tm)], b_scr, sem).wait()

    x2 = x1_ref[...] + (g1 * a_scr[...] + g2 * b_scr[...])
    ms = jnp.mean(x2 * x2, axis=-1, keepdims=True)
    y = (x2 * lax.rsqrt(ms + EPS)) * g_ref[...]

    @pl.when(i < tiles_p)
    def _():
        op_ref[...] = y

    @pl.when(i >= tiles_p)
    def _():
        os_ref[...] = y


def _combine(slot1, slot2, rt, x1, g_final, ys, n_p):
    n = x1.shape[0]
    tm = TOKEN_TILE
    tiles_p = n_p // tm
    smem = lambda: pl.BlockSpec((tm,), lambda i: (i,), memory_space=pltpu.SMEM)
    return pl.pallas_call(
        functools.partial(_combine_kernel, tiles_p=tiles_p),
        grid=(n // tm,),
        in_specs=[
            smem(), smem(),
            pl.BlockSpec((SUBLANES, tm), lambda i: (0, i)),
            pl.BlockSpec((tm, D_MODEL), lambda i: (i, 0)),
            pl.BlockSpec((1, D_MODEL), lambda i: (0, 0)),
            pl.BlockSpec(memory_space=pl.ANY),
        ],
        out_specs=[
            pl.BlockSpec((tm, D_MODEL), lambda i: (jnp.minimum(i, tiles_p - 1), 0)),
            pl.BlockSpec((tm, D_MODEL), lambda i: (jnp.maximum(i - tiles_p, 0), 0)),
        ],
        out_shape=[
            jax.ShapeDtypeStruct((n_p, D_MODEL), F32),
            jax.ShapeDtypeStruct((n - n_p, D_MODEL), F32),
        ],
        scratch_shapes=[
            pltpu.VMEM((tm, D_MODEL), F32),
            pltpu.VMEM((tm, D_MODEL), F32),
            pltpu.SemaphoreType.DMA(()),
        ],
        compiler_params=_cparams(("arbitrary",)),
        name="moe_combine_norm",
    )(slot1, slot2, rt, x1, g_final, ys)


def _rope_tables(seq_max):
    half = ROT_DIM // 2
    inv = ROPE_THETA ** (-jnp.arange(0, ROT_DIM, 2, dtype=F32) / ROT_DIM)
    ang = jnp.arange(seq_max, dtype=F32)[:, None] * inv[None, :]
    cos, sin = jnp.cos(ang), jnp.sin(ang)
    ones = jnp.ones((seq_max, DIFF_HEAD_DIM - ROT_DIM), F32)
    zeros = jnp.zeros((seq_max, DIFF_HEAD_DIM - ROT_DIM), F32)
    z8 = jnp.zeros((seq_max, half), F32)
    ct = jnp.concatenate([cos, cos, ones], axis=1)
    sa = jnp.concatenate([-sin, z8, zeros], axis=1)
    sb = jnp.concatenate([z8, sin, zeros], axis=1)
    reps = 2 * LANES // DIFF_HEAD_DIM
    return tuple(jnp.tile(t, (1, reps)) for t in (ct, sa, sb))


def _channel_dft_matrix():
    c = np.arange(FOURIER_GROUP_DIM)
    ang = 2.0 * np.pi * ((c[:, None] * c[None, :]) % FOURIER_GROUP_DIM) / FOURIER_GROUP_DIM
    scale = FOURIER_GROUP_DIM ** -0.5
    eye = np.eye(FOURIER_GROUPS)
    bd = np.concatenate([np.kron(eye, np.cos(ang) * scale), np.kron(eye, np.sin(ang) * scale)], axis=1)
    return jnp.asarray(bd, dtype=BF16)


def _position_dft_matrices(seq):
    lo = FOURIER_GROUP_DIM
    hi = seq // lo
    sp = jnp.arange(seq, dtype=jnp.int32)[None, :]
    a = jnp.arange(hi, dtype=jnp.int32)[:, None]
    b = jnp.arange(lo, dtype=jnp.int32)[:, None]
    w = 2.0 * math.pi / seq
    ang_a = ((a * lo * sp) % seq).astype(F32) * w
    ang_b = ((b * sp) % seq).astype(F32) * w
    ca, sa = jnp.cos(ang_a)[:, None, :], jnp.sin(ang_a)[:, None, :]
    cb, sb = jnp.cos(ang_b)[None, :, :], jnp.sin(ang_b)[None, :, :]
    cmat = (ca * cb - sa * sb).reshape(seq, seq).astype(BF16)
    smat = (-(sa * cb + ca * sb)).reshape(seq, seq).astype(BF16)
    return cmat, smat


def kernel(x_prompt, x_sample, norm_mix, w_in, lambda_qk, subln_gain, w_out, norm_ffn, ffn_w_gate_up,
           ffn_w_down, router_w, expert_w_gate_up, expert_w_down, final_norm):
    nb_p, seq_p, _ = x_prompt.shape
    nb_s, seq_s, _ = x_sample.shape
    n_p, n_s = nb_p * seq_p, nb_s * seq_s
    n = n_p + n_s
    tm = TOKEN_TILE
    assert seq_p % tm == 0 and seq_s % tm == 0
    assert n_p % seq_s == 0 and seq_p % ATTN_K_TILE == 0 and seq_s % ATTN_K_TILE == 0
    assert n_p % min(seq_s, DFT_K_TILE) == 0
    assert DEPTH == 2

    ct, sa, sb = _rope_tables(max(seq_p, seq_s))
    bd = _channel_dft_matrix()
    dft_p = _position_dft_matrices(seq_p)
    dft_s = dft_p if seq_s == seq_p else _position_dft_matrices(seq_s)

    tiles_p = n_p // tm
    two_arrays = _Split(tiles_p, 0)
    one_array = _Split(tiles_p, tiles_p)
    xa, xb, x_split = x_prompt.reshape(n_p, D_MODEL), x_sample.reshape(n_s, D_MODEL), two_arrays
    out = None
    for layer in range(DEPTH):
        lam_init = 0.8 - 0.6 * math.exp(-0.3 * layer)
        yf, qkv = _norm_inproj(xa, xb, x_split, n, norm_mix[layer][None, :], w_in[layer].astype(BF16),
                               bd, ct, sa, sb, seq_p, seq_s)
        gain_col = subln_gain[layer][:, None]
        o_a = (_attention(qkv, lambda_qk[layer], gain_col, 0, nb_p, seq_p, lam_init),
               _attention(qkv, lambda_qk[layer], gain_col, n_p, nb_s, seq_s, lam_init))
        o_f = (_fourier(yf, dft_p[0], dft_p[1], 0, nb_p, seq_p),
               _fourier(yf, dft_s[0], dft_s[1], n_p, nb_s, seq_s))
        x1, h = _outproj(xa, xb, x_split, o_f, o_a, two_arrays, n, w_out[layer].astype(BF16),
                         norm_ffn[layer][None, :])
        if layer % 2 == 0:
            x = _dense_ffn(x1, h, ffn_w_gate_up[layer // 2].astype(BF16), ffn_w_down[layer // 2].astype(BF16))
            xa, xb, x_split = x, x, one_array
        else:
            j = layer // 2
            rt, cnt = _router(x1, norm_ffn[layer][None, :], router_w[j].T)
            te = EXPERT_TILE
            counts = cnt[:, 0].astype(jnp.int32)
            padded = ((counts + te - 1) // te) * te
            ends = jnp.cumsum(padded)
            offs = ends - padded
            e1, e2 = rt[0].astype(jnp.int32), rt[1].astype(jnp.int32)
            slot1 = offs[e1] + rt[2].astype(jnp.int32)
            slot2 = offs[e2] + rt[3].astype(jnp.int32)
            n_slots = 2 * n + N_EXPERTS * te
            starts = jnp.arange(n_slots // te, dtype=jnp.int32) * te
            tile_expert = jnp.minimum(jnp.sum((starts[:, None] >= ends[None, :]).astype(jnp.int32), axis=1),
                                      N_EXPERTS - 1)
            tile_valid = jnp.clip(counts[tile_expert] - (starts - offs[tile_expert]), 0, te).astype(jnp.int32)
            n_tiles = (ends[-1:] // te).astype(jnp.int32)
            xs = _dispatch(slot1, slot2, h, n_slots)
            ys = _experts(tile_expert, n_tiles, tile_valid, xs,
                          expert_w_gate_up[j].astype(BF16), expert_w_down[j].astype(BF16))
            out = _combine(slot1, slot2, rt, x1, final_norm[None, :], ys, n_p)
    y_p, y_s = out
    return (y_p.reshape(nb_p, seq_p, D_MODEL), y_s.reshape(nb_s, seq_s, D_MODEL))
```

```python
import functools
import math

import numpy as np
import jax
import jax.numpy as jnp
from jax import lax
from jax.experimental import pallas as pl
from jax.experimental.pallas import tpu as pltpu

F32 = jnp.float32
BF16 = jnp.bfloat16

D_MODEL = 1024
DEPTH = 2
FOURIER_WIDTH = 256
FOURIER_GROUPS = 4
FOURIER_GROUP_DIM = 64
ATTN_WIDTH = 768
DIFF_HEAD_DIM = 64
DIFF_V_DIM = 128
N_DIFF_HEADS = 6
QK_WIDTH = 768
IN_PROJ_WIDTH = 2560
ROT_DIM = 16
ROPE_THETA = 500000.0
FF_DENSE = 2816
N_EXPERTS = 8
FF_EXPERT = 3584
EPS = 1e-5

LANES = 128
SUBLANES = 8
VMEM_LIMIT_BYTES = 56 * 1024 * 1024

TOKEN_TILE = 512
ATTN_Q_TILE = 512
ATTN_K_TILE = 512
ATTN_K_SUPER = 2048
LOG2E = 1.4426950408889634
DFT_ROW_TILE = 1024
DFT_K_TILE = 2048
FFN_CHUNK = 256
EXPERT_TILE = 1024
EXPERT_CHUNK = 1792
ROW_DMA_UNROLL = 8


def _cparams(sem):
    return pltpu.CompilerParams(dimension_semantics=sem, vmem_limit_bytes=VMEM_LIMIT_BYTES)


def _aligned(start, multiple):
    return start if isinstance(start, int) else pl.multiple_of(start, multiple)


def _silu_mul(g, u):
    return (g * (1.0 / (1.0 + jnp.exp(-g)))) * u


class _Split:
    def __init__(self, tiles_a, b_blk0):
        self.tiles_a, self.b_blk0 = tiles_a, b_blk0

    def specs(self, tm, width):
        ta, b0 = self.tiles_a, self.b_blk0
        return [pl.BlockSpec((tm, width), lambda i: (jnp.minimum(i, ta - 1), 0)),
                pl.BlockSpec((tm, width), lambda i: (b0 + jnp.maximum(i - ta, 0), 0))]

    def pick(self, a_ref, b_ref):
        return jnp.where(pl.program_id(0) < self.tiles_a, a_ref[...], b_ref[...])


def _norm_inproj_kernel(xa_ref, xb_ref, g_ref, w_ref, bd_ref, ct_ref, sa_ref, sb_ref, yf_ref, qkv_ref, h_scr,
                        *, split):
    x = split.pick(xa_ref, xb_ref)
    ms = jnp.mean(x * x, axis=-1, keepdims=True)
    h_scr[...] = ((x * lax.rsqrt(ms + EPS)) * g_ref[...]).astype(BF16)
    hb = h_scr[...]
    uf = jnp.dot(hb, w_ref[:, 0:FOURIER_WIDTH], preferred_element_type=F32)
    yf_ref[...] = jnp.dot(uf.astype(BF16), bd_ref[...], preferred_element_type=F32).astype(BF16)
    ct, sa, sb = ct_ref[...], sa_ref[...], sb_ref[...]
    width = 2 * LANES
    for blk in range(2 * QK_WIDTH // width):
        c0 = FOURIER_WIDTH + blk * width
        t = jnp.dot(hb, w_ref[:, c0:c0 + width], preferred_element_type=F32)
        r = t * ct + pltpu.roll(t, width - ROT_DIM // 2, 1) * sa + pltpu.roll(t, ROT_DIM // 2, 1) * sb
        if blk < QK_WIDTH // width:
            r = r * (DIFF_HEAD_DIM ** -0.5 * LOG2E)
        qkv_ref[:, blk * width:(blk + 1) * width] = r.astype(BF16)
    v0 = FOURIER_WIDTH + 2 * QK_WIDTH
    for blk in range(ATTN_WIDTH // width):
        t = jnp.dot(hb, w_ref[:, v0 + blk * width:v0 + (blk + 1) * width], preferred_element_type=F32)
        qkv_ref[:, 2 * QK_WIDTH + blk * width:2 * QK_WIDTH + (blk + 1) * width] = t.astype(BF16)


def _norm_inproj(xa, xb, split, n, g, w_bf, bd, ct, sa, sb, seq_p, seq_s):
    tm = TOKEN_TILE
    tiles_p = split.tiles_a
    per_p, per_s = seq_p // tm, seq_s // tm

    def pos_map(i):
        return (jnp.where(i < tiles_p, i % per_p, (i - tiles_p) % per_s), 0)

    tab = pl.BlockSpec((tm, 2 * LANES), pos_map)
    return pl.pallas_call(
        functools.partial(_norm_inproj_kernel, split=split),
        grid=(n // tm,),
        in_specs=split.specs(tm, D_MODEL) + [
            pl.BlockSpec((1, D_MODEL), lambda i: (0, 0)),
            pl.BlockSpec((D_MODEL, IN_PROJ_WIDTH), lambda i: (0, 0)),
            pl.BlockSpec((FOURIER_WIDTH, 2 * FOURIER_WIDTH), lambda i: (0, 0)),
            tab, tab, tab,
        ],
        out_specs=[
            pl.BlockSpec((tm, 2 * FOURIER_WIDTH), lambda i: (i, 0)),
            pl.BlockSpec((tm, 2 * QK_WIDTH + ATTN_WIDTH), lambda i: (i, 0)),
        ],
        out_shape=[
            jax.ShapeDtypeStruct((n, 2 * FOURIER_WIDTH), BF16),
            jax.ShapeDtypeStruct((n, 2 * QK_WIDTH + ATTN_WIDTH), BF16),
        ],
        scratch_shapes=[pltpu.VMEM((tm, D_MODEL), BF16)],
        compiler_params=_cparams(("arbitrary",)),
        name="norm_inproj",
    )(xa, xb, g, w_bf, bd, ct, sa, sb)


def _attn_kernel(lq_ref, q_ref, k_ref, v_ref, g_ref, o_ref, vt_scr, sa_scr, sb_scr, p_scr, acc_scr, *,
                 seq, lam_init):
    tq, tk = ATTN_Q_TILE, ATTN_K_TILE
    ks = min(seq, ATTN_K_SUPER)
    nsb, nkc = seq // ks, ks // tk
    groups = tk // SUBLANES
    n_units = (seq // tq) * nsb
    assert n_units % 2 == 0
    s_bufs = (sa_scr, sb_scr)
    for sb in range(nsb):
        for j in range(nkc):
            r0 = sb * ks + j * tk
            vt_scr[sb, :, j * tk:(j + 1) * tk] = v_ref[r0:r0 + tk, :].astype(F32).T.astype(BF16)

    lq = lq_ref[...]
    lam = (jnp.exp(jnp.sum(lq[0:1, :] * lq[1:2, :], axis=1, keepdims=True))
           - jnp.exp(jnp.sum(lq[2:3, :] * lq[3:4, :], axis=1, keepdims=True)) + lam_init)
    gain = g_ref[...]
    first = lax.broadcasted_iota(jnp.int32, (1, DIFF_V_DIM), 1) < DIFF_HEAD_DIM
    nt_dims = (((1,), (1,)), ((), ()))
    comps = (0, 1)

    def load_q(u):
        q = q_ref[pl.ds(_aligned((u // nsb) * tq, tq), tq), :]
        return (jnp.where(first, q, jnp.zeros_like(q)), jnp.where(first, jnp.zeros_like(q), q))

    def score_chunk(u, qc, s_scr, j, mx):
        kblk = k_ref[pl.ds(_aligned((u % nsb) * ks + j * tk, tk), tk), :]
        out = []
        for c in comps:
            s = lax.dot_general(kblk, qc[c], nt_dims, preferred_element_type=F32)
            s_scr[c, j] = s
            out.append(jnp.maximum(mx[c], jnp.max(s.reshape(groups, SUBLANES, tq), axis=0)))
        return out

    def exp_chunk(s_scr, j, m_new, ls):
        out = []
        for c in comps:
            p = jnp.exp2(s_scr[c, j] - m_new[c])
            out.append(ls[c] + jnp.sum(p.reshape(groups, SUBLANES, tq), axis=0))
            p_scr[c, j * tk:(j + 1) * tk, :] = p.astype(BF16)
        return out

    def unit(u, slot, st, with_next):
        mx, m, l = st[0:2], st[2:4], st[4:6]
        sb, qi = u % nsb, u // nsb
        m_new = [jnp.maximum(m[c], jnp.max(mx[c], axis=0, keepdims=True)) for c in comps]
        alpha = [jnp.exp2(m[c] - m_new[c]) for c in comps]
        ls = [jnp.zeros((SUBLANES, tq), F32) for _ in comps]
        mx_next = [jnp.full((SUBLANES, tq), -jnp.inf, F32) for _ in comps]
        qc_next = load_q(u + 1) if with_next else None
        for j in range(nkc):
            if with_next:
                mx_next = score_chunk(u + 1, qc_next, s_bufs[1 - slot], j, mx_next)
            ls = exp_chunk(s_bufs[slot], j, m_new, ls)
        l_new = [alpha[c] * l[c] + jnp.sum(ls[c], axis=0, keepdims=True) for c in comps]
        vt = vt_scr[sb]
        for c in comps:
            acc_scr[c] = alpha[c] * acc_scr[c] + jnp.dot(vt, p_scr[c], preferred_element_type=F32)

        last = jnp.asarray(sb == nsb - 1)

        @pl.when(last)
        def _():
            o_t = acc_scr[0] * (1.0 / l_new[0]) - lam * (acc_scr[1] * (1.0 / l_new[1]))
            ms = jnp.mean(o_t * o_t, axis=0, keepdims=True)
            y_t = ((o_t * lax.rsqrt(ms + EPS)) * gain) * (1.0 - lam_init)
            o_ref[pl.ds(_aligned(qi * tq, tq), tq), :] = y_t.T.astype(BF16)
            acc_scr[...] = jnp.zeros_like(acc_scr)

        m_out = [jnp.where(last, -jnp.inf, m_new[c]) for c in comps]
        l_out = [jnp.where(last, 0.0, l_new[c]) for c in comps]
        return (mx_next[0], mx_next[1], m_out[0], m_out[1], l_out[0], l_out[1])

    acc_scr[...] = jnp.zeros_like(acc_scr)
    mx0 = [jnp.full((SUBLANES, tq), -jnp.inf, F32) for _ in comps]
    qc0 = load_q(0)
    for j in range(nkc):
        mx0 = score_chunk(0, qc0, s_bufs[0], j, mx0)
    neg = jnp.full((1, tq), -jnp.inf, F32)
    zero = jnp.zeros((1, tq), F32)
    st = (mx0[0], mx0[1], neg, neg, zero, zero)

    def pair(i, st):
        st = unit(2 * i, 0, st, True)
        return unit(2 * i + 1, 1, st, True)

    st = lax.fori_loop(0, n_units // 2 - 1, pair, st)
    st = unit(n_units - 2, 0, st, True)
    unit(n_units - 1, 1, st, False)


def _attention(qkv, lq, gain_col, row0, nb, seq, lam_init):
    blk0 = row0 // seq
    nh = N_DIFF_HEADS
    tq, tk = ATTN_Q_TILE, ATTN_K_TILE
    ks = min(seq, ATTN_K_SUPER)
    kern = functools.partial(_attn_kernel, seq=seq, lam_init=lam_init)
    return pl.pallas_call(
        kern,
        grid=(nb, nh),
        in_specs=[
            pl.BlockSpec((4, DIFF_HEAD_DIM), lambda b, h: (0, 0)),
            pl.BlockSpec((seq, LANES), lambda b, h: (blk0 + b, h)),
            pl.BlockSpec((seq, LANES), lambda b, h: (blk0 + b, nh + h)),
            pl.BlockSpec((seq, LANES), lambda b, h: (blk0 + b, 2 * nh + h)),
            pl.BlockSpec((DIFF_V_DIM, 1), lambda b, h: (0, 0)),
        ],
        out_specs=pl.BlockSpec((seq, LANES), lambda b, h: (b, h)),
        out_shape=jax.ShapeDtypeStruct((nb * seq, ATTN_WIDTH), BF16),
        scratch_shapes=[
            pltpu.VMEM((seq // ks, DIFF_V_DIM, ks), BF16),
            pltpu.VMEM((2, ks // tk, tk, tq), F32),
            pltpu.VMEM((2, ks // tk, tk, tq), F32),
            pltpu.VMEM((2, ks, tq), BF16),
            pltpu.VMEM((2, DIFF_V_DIM, tq), F32),
        ],
        compiler_params=_cparams(("arbitrary", "arbitrary")),
        name="diff_attention",
    )(lq, qkv, qkv, qkv, gain_col)


def _dft_kernel(c_ref, s_ref, y_ref, o_ref, acc_scr, *, nk, scale):
    k = pl.program_id(2)
    y = y_ref[...]
    part = (jnp.dot(c_ref[...], y[:, 0:FOURIER_WIDTH], preferred_element_type=F32)
            + jnp.dot(s_ref[...], y[:, FOURIER_WIDTH:2 * FOURIER_WIDTH], preferred_element_type=F32))

    @pl.when(k == 0)
    def _():
        acc_scr[...] = part

    @pl.when(k > 0)
    def _():
        acc_scr[...] += part

    @pl.when(k == nk - 1)
    def _():
        o_ref[...] = (acc_scr[...] * scale).astype(BF16)


def _fourier(yf, cmat, smat, row0, nb, seq):
    tm = min(seq, DFT_ROW_TILE)
    tk = min(seq, DFT_K_TILE)
    ni, nk = seq // tm, seq // tk
    yblk0 = row0 // tk
    kern = functools.partial(_dft_kernel, nk=nk, scale=1.0 / math.sqrt(seq))
    return pl.pallas_call(
        kern,
        grid=(ni, nb, nk),
        in_specs=[
            pl.BlockSpec((tm, tk), lambda i, b, k: (i, k)),
            pl.BlockSpec((tm, tk), lambda i, b, k: (i, k)),
            pl.BlockSpec((tk, 2 * FOURIER_WIDTH), lambda i, b, k: (yblk0 + b * nk + k, 0)),
        ],
        out_specs=pl.BlockSpec((tm, FOURIER_WIDTH), lambda i, b, k: (b * ni + i, 0)),
        out_shape=jax.ShapeDtypeStruct((nb * seq, FOURIER_WIDTH), BF16),
        scratch_shapes=[pltpu.VMEM((tm, FOURIER_WIDTH), F32)],
        compiler_params=_cparams(("arbitrary", "arbitrary", "arbitrary")),
        name="fourier_dft",
    )(cmat, smat, yf)


def _outproj_kernel(xa_ref, xb_ref, ofa_ref, ofb_ref, oaa_ref, oab_ref, w_ref, g_ref, x1_ref, h_ref, *,
                    x_split, o_split):
    y = (jnp.dot(o_split.pick(ofa_ref, ofb_ref), w_ref[0:FOURIER_WIDTH, :], preferred_element_type=F32)
         + jnp.dot(o_split.pick(oaa_ref, oab_ref), w_ref[FOURIER_WIDTH:D_MODEL, :], preferred_element_type=F32))
    x1 = x_split.pick(xa_ref, xb_ref) + y
    x1_ref[...] = x1
    ms = jnp.mean(x1 * x1, axis=-1, keepdims=True)
    h_ref[...] = ((x1 * lax.rsqrt(ms + EPS)) * g_ref[...]).astype(BF16)


def _outproj(xa, xb, x_split, o_f, o_a, o_split, n, w_bf, g):
    tm = TOKEN_TILE
    return pl.pallas_call(
        functools.partial(_outproj_kernel, x_split=x_split, o_split=o_split),
        grid=(n // tm,),
        in_specs=(x_split.specs(tm, D_MODEL) + o_split.specs(tm, FOURIER_WIDTH) + o_split.specs(tm, ATTN_WIDTH) + [
            pl.BlockSpec((D_MODEL, D_MODEL), lambda i: (0, 0)),
            pl.BlockSpec((1, D_MODEL), lambda i: (0, 0)),
        ]),
        out_specs=[
            pl.BlockSpec((tm, D_MODEL), lambda i: (i, 0)),
            pl.BlockSpec((tm, D_MODEL), lambda i: (i, 0)),
        ],
        out_shape=[
            jax.ShapeDtypeStruct((n, D_MODEL), F32),
            jax.ShapeDtypeStruct((n, D_MODEL), BF16),
        ],
        compiler_params=_cparams(("arbitrary",)),
        name="outproj_norm",
    )(xa, xb, o_f[0], o_f[1], o_a[0], o_a[1], w_bf, g)


def _ffn_kernel(x1_ref, h_ref, wgu_ref, wd_ref, o_ref, acc_scr):
    hb = h_ref[...]
    fc = FFN_CHUNK
    for j in range(FF_DENSE // fc):
        g = jnp.dot(hb, wgu_ref[:, j * fc:(j + 1) * fc], preferred_element_type=F32)
        u = jnp.dot(hb, wgu_ref[:, FF_DENSE + j * fc:FF_DENSE + (j + 1) * fc], preferred_element_type=F32)
        c = jnp.dot(_silu_mul(g, u).astype(BF16), wd_ref[j * fc:(j + 1) * fc, :], preferred_element_type=F32)
        if j == 0:
            acc_scr[...] = c
        else:
            acc_scr[...] += c
    o_ref[...] = x1_ref[...] + acc_scr[...]


def _dense_ffn(x1, h, wgu_bf, wd_bf):
    n = x1.shape[0]
    tm = TOKEN_TILE
    return pl.pallas_call(
        _ffn_kernel,
        grid=(n // tm,),
        in_specs=[
            pl.BlockSpec((tm, D_MODEL), lambda i: (i, 0)),
            pl.BlockSpec((tm, D_MODEL), lambda i: (i, 0)),
            pl.BlockSpec((D_MODEL, 2 * FF_DENSE), lambda i: (0, 0)),
            pl.BlockSpec((FF_DENSE, D_MODEL), lambda i: (0, 0)),
        ],
        out_specs=pl.BlockSpec((tm, D_MODEL), lambda i: (i, 0)),
        out_shape=jax.ShapeDtypeStruct((n, D_MODEL), F32),
        scratch_shapes=[pltpu.VMEM((tm, D_MODEL), F32)],
        compiler_params=_cparams(("arbitrary",)),
        name="dense_swiglu",
    )(x1, h, wgu_bf, wd_bf)


def _router_kernel(x_ref, g_ref, rw_ref, rt_ref, cnt_ref, carry_scr):
    i = pl.program_id(0)
    tm = TOKEN_TILE

    @pl.when(i == 0)
    def _():
        carry_scr[...] = jnp.zeros_like(carry_scr)

    x = x_ref[...]
    ms = jnp.mean(x * x, axis=-1, keepdims=True)
    h = (x * lax.rsqrt(ms + EPS)) * g_ref[...]
    h_hi = h.astype(BF16)
    h_lo = (h - h_hi.astype(F32)).astype(BF16)
    rw = rw_ref[...]
    rw_hi = rw.astype(BF16)
    rw_lo = (rw - rw_hi.astype(F32)).astype(BF16)
    nt_dims = (((1,), (1,)), ((), ()))
    dg = lambda a, b: lax.dot_general(a, b, nt_dims, preferred_element_type=F32)
    logits = dg(rw_hi, h_hi) + (dg(rw_hi, h_lo) + dg(rw_lo, h_hi))

    eidx = lax.broadcasted_iota(jnp.int32, (N_EXPERTS, tm), 0)
    m1 = jnp.max(logits, axis=0, keepdims=True)
    i1 = jnp.min(jnp.where(logits == m1, eidx, N_EXPERTS), axis=0, keepdims=True)
    oh1 = eidx == i1
    rest = jnp.where(oh1, -jnp.inf, logits)
    m2 = jnp.max(rest, axis=0, keepdims=True)
    i2 = jnp.min(jnp.where(rest == m2, eidx, N_EXPERTS), axis=0, keepdims=True)
    oh2 = eidx == i2
    e = jnp.exp(m2 - m1)
    g1 = 1.0 / (1.0 + e)
    g2 = e / (1.0 + e)

    onehot = jnp.where(oh1 | oh2, 1.0, 0.0)
    upper = (lax.broadcasted_iota(jnp.int32, (tm, tm), 0)
             < lax.broadcasted_iota(jnp.int32, (tm, tm), 1))
    before = jnp.dot(onehot.astype(BF16), jnp.where(upper, 1.0, 0.0).astype(BF16),
                     preferred_element_type=F32)
    rank = before + carry_scr[:, 0:1]
    r1 = jnp.sum(jnp.where(oh1, rank, 0.0), axis=0, keepdims=True)
    r2 = jnp.sum(jnp.where(oh2, rank, 0.0), axis=0, keepdims=True)
    zero = jnp.zeros_like(g1)
    rt_ref[...] = jnp.concatenate(
        [i1.astype(F32), i2.astype(F32), r1, r2, g1, g2, zero, zero], axis=0)
    total = carry_scr[...] + jnp.sum(onehot, axis=1, keepdims=True)
    carry_scr[...] = total
    cnt_ref[...] = total


def _router(x1, g, rw_t):
    n = x1.shape[0]
    tm = TOKEN_TILE
    return pl.pallas_call(
        _router_kernel,
        grid=(n // tm,),
        in_specs=[
            pl.BlockSpec((tm, D_MODEL), lambda i: (i, 0)),
            pl.BlockSpec((1, D_MODEL), lambda i: (0, 0)),
            pl.BlockSpec((N_EXPERTS, D_MODEL), lambda i: (0, 0)),
        ],
        out_specs=[
            pl.BlockSpec((SUBLANES, tm), lambda i: (0, i)),
            pl.BlockSpec((N_EXPERTS, LANES), lambda i: (0, 0)),
        ],
        out_shape=[
            jax.ShapeDtypeStruct((SUBLANES, n), F32),
            jax.ShapeDtypeStruct((N_EXPERTS, LANES), F32),
        ],
        scratch_shapes=[pltpu.VMEM((N_EXPERTS, LANES), F32)],
        compiler_params=_cparams(("arbitrary",)),
        name="router_top2",
    )(x1, g, rw_t)


def _row_copy(src, src_row, dst, dst_row, sem):
    return pltpu.make_async_copy(src.at[pl.ds(src_row, 1)], dst.at[pl.ds(dst_row, 1)], sem)


def _dispatch_kernel(s1_ref, s2_ref, h_ref, xs_ref, rows_scr, sem):
    tm = TOKEN_TILE
    rows_scr[...] = h_ref[...].astype(F32)

    def issue(t, c):
        _row_copy(rows_scr, t, xs_ref, s1_ref[t], sem).start()
        _row_copy(rows_scr, t, xs_ref, s2_ref[t], sem).start()
        return c

    lax.fori_loop(0, tm, issue, 0, unroll=ROW_DMA_UNROLL)
    for _ in range(2):
        pltpu.make_async_copy(rows_scr, xs_ref.at[pl.ds(0, tm)], sem).wait()


def _dispatch(slot1, slot2, h, n_slots):
    n = h.shape[0]
    tm = TOKEN_TILE
    smem = lambda: pl.BlockSpec((tm,), lambda i: (i,), memory_space=pltpu.SMEM)
    return pl.pallas_call(
        _dispatch_kernel,
        grid=(n // tm,),
        in_specs=[smem(), smem(), pl.BlockSpec((tm, D_MODEL), lambda i: (i, 0))],
        out_specs=pl.BlockSpec(memory_space=pl.ANY),
        out_shape=jax.ShapeDtypeStruct((n_slots, D_MODEL), F32),
        scratch_shapes=[pltpu.VMEM((tm, D_MODEL), F32), pltpu.SemaphoreType.DMA(())],
        compiler_params=_cparams(("arbitrary",)),
        name="moe_dispatch",
    )(slot1, slot2, h)


def _expert_kernel(te_ref, nt_ref, tv_ref, xs_ref, wg_ref, wu_ref, wd_ref, ys_ref, xb_scr):
    i = pl.program_id(0)
    j = pl.program_id(1)

    @pl.when(i < nt_ref[0])
    def _():
        @pl.when(j == 0)
        def _():
            rows = lax.broadcasted_iota(jnp.int32, (EXPERT_TILE, 1), 0)
            xb_scr[...] = jnp.where(rows < tv_ref[i], xs_ref[...], 0.0).astype(BF16)
            ys_ref[...] = jnp.zeros_like(ys_ref)

        xb = xb_scr[...]
        fc = FFN_CHUNK
        for t in range(EXPERT_CHUNK // fc):
            g = jnp.dot(xb, wg_ref[0, :, t * fc:(t + 1) * fc], preferred_element_type=F32)
            u = jnp.dot(xb, wu_ref[0, :, t * fc:(t + 1) * fc], preferred_element_type=F32)
            ys_ref[...] += jnp.dot(_silu_mul(g, u).astype(BF16), wd_ref[0, t * fc:(t + 1) * fc, :],
                                   preferred_element_type=F32)


def _experts(tile_expert, n_tiles, tile_valid, xs, wgu_bf, wd_bf):
    n_slots = xs.shape[0]
    te, fc = EXPERT_TILE, EXPERT_CHUNK
    nj = FF_EXPERT // fc
    max_tiles = n_slots // te

    def tile(i, nt):
        return jnp.minimum(i, nt[0] - 1)

    def chunk(i, j, nt):
        return jnp.where(i < nt[0], j, nj - 1)

    grid_spec = pltpu.PrefetchScalarGridSpec(
        num_scalar_prefetch=3,
        grid=(max_tiles, nj),
        in_specs=[
            pl.BlockSpec((te, D_MODEL), lambda i, j, e, nt, tv: (tile(i, nt), 0)),
            pl.BlockSpec((1, D_MODEL, fc), lambda i, j, e, nt, tv: (e[tile(i, nt)], 0, chunk(i, j, nt))),
            pl.BlockSpec((1, D_MODEL, fc), lambda i, j, e, nt, tv: (e[tile(i, nt)], 0, nj + chunk(i, j, nt))),
            pl.BlockSpec((1, fc, D_MODEL), lambda i, j, e, nt, tv: (e[tile(i, nt)], chunk(i, j, nt), 0)),
        ],
        out_specs=pl.BlockSpec((te, D_MODEL), lambda i, j, e, nt, tv: (tile(i, nt), 0)),
        scratch_shapes=[pltpu.VMEM((te, D_MODEL), BF16)],
    )
    return pl.pallas_call(
        _expert_kernel,
        grid_spec=grid_spec,
        out_shape=jax.ShapeDtypeStruct((n_slots, D_MODEL), F32),
        compiler_params=_cparams(("arbitrary", "arbitrary")),
        name="expert_swiglu",
    )(tile_expert, n_tiles, tile_valid, xs, wgu_bf, wgu_bf, wd_bf)


def _combine_kernel(s1_ref, s2_ref, rt_ref, x1_ref, g_ref, ys_ref, op_ref, os_ref, a_scr, b_scr, sem, *,
                    tiles_p):
    tm = TOKEN_TILE
    i = pl.program_id(0)

    def issue(t, c):
        _row_copy(ys_ref, s1_ref[t], a_scr, t, sem).start()
        _row_copy(ys_ref, s2_ref[t], b_scr, t, sem).start()
        return c

    lax.fori_loop(0, tm, issue, 0, unroll=ROW_DMA_UNROLL)

    rt = jnp.concatenate([rt_ref[...], jnp.zeros((LANES - SUBLANES, tm), F32)], axis=0)
    rt_t = rt.T
    g1 = rt_t[:, 4:5]
    g2 = rt_t[:, 5:6]

    pltpu.make_async_copy(ys_ref.at[pl.ds(0, tm)], a_scr, sem).wait()
    pltpu.make_async_copy(ys_ref.at[pl.ds(0, tm)], b_scr, sem).wait()

    x2 = x1_ref[...] + (g1 * a_scr[...] + g2 * b_scr[...])
    ms = jnp.mean(x2 * x2, axis=-1, keepdims=True)
    y = (x2 * lax.rsqrt(ms + EPS)) * g_ref[...]

    @pl.when(i < tiles_p)
    def _():
        op_ref[...] = y

    @pl.when(i >= tiles_p)
    def _():
        os_ref[...] = y


def _combine(slot1, slot2, rt, x1, g_final, ys, n_p):
    n = x1.shape[0]
    tm = TOKEN_TILE
    tiles_p = n_p // tm
    smem = lambda: pl.BlockSpec((tm,), lambda i: (i,), memory_space=pltpu.SMEM)
    return pl.pallas_call(
        functools.partial(_combine_kernel, tiles_p=tiles_p),
        grid=(n // tm,),
        in_specs=[
            smem(), smem(),
            pl.BlockSpec((SUBLANES, tm), lambda i: (0, i)),
            pl.BlockSpec((tm, D_MODEL), lambda i: (i, 0)),
            pl.BlockSpec((1, D_MODEL), lambda i: (0, 0)),
            pl.BlockSpec(memory_space=pl.ANY),
        ],
        out_specs=[
            pl.BlockSpec((tm, D_MODEL), lambda i: (jnp.minimum(i, tiles_p - 1), 0)),
            pl.BlockSpec((tm, D_MODEL), lambda i: (jnp.maximum(i - tiles_p, 0), 0)),
        ],
        out_shape=[
            jax.ShapeDtypeStruct((n_p, D_MODEL), F32),
            jax.ShapeDtypeStruct((n - n_p, D_MODEL), F32),
        ],
        scratch_shapes=[
            pltpu.VMEM((tm, D_MODEL), F32),
            pltpu.VMEM((tm, D_MODEL), F32),
            pltpu.SemaphoreType.DMA(()),
        ],
        compiler_params=_cparams(("arbitrary",)),
        name="moe_combine_norm",
    )(slot1, slot2, rt, x1, g_final, ys)


def _rope_tables(seq_max):
    half = ROT_DIM // 2
    inv = ROPE_THETA ** (-jnp.arange(0, ROT_DIM, 2, dtype=F32) / ROT_DIM)
    ang = jnp.arange(seq_max, dtype=F32)[:, None] * inv[None, :]
    cos, sin = jnp.cos(ang), jnp.sin(ang)
    ones = jnp.ones((seq_max, DIFF_HEAD_DIM - ROT_DIM), F32)
    zeros = jnp.zeros((seq_max, DIFF_HEAD_DIM - ROT_DIM), F32)
    z8 = jnp.zeros((seq_max, half), F32)
    ct = jnp.concatenate([cos, cos, ones], axis=1)
    sa = jnp.concatenate([-sin, z8, zeros], axis=1)
    sb = jnp.concatenate([z8, sin, zeros], axis=1)
    reps = 2 * LANES // DIFF_HEAD_DIM
    return tuple(jnp.tile(t, (1, reps)) for t in (ct, sa, sb))


def _channel_dft_matrix():
    c = np.arange(FOURIER_GROUP_DIM)
    ang = 2.0 * np.pi * ((c[:, None] * c[None, :]) % FOURIER_GROUP_DIM) / FOURIER_GROUP_DIM
    scale = FOURIER_GROUP_DIM ** -0.5
    eye = np.eye(FOURIER_GROUPS)
    bd = np.concatenate([np.kron(eye, np.cos(ang) * scale), np.kron(eye, np.sin(ang) * scale)], axis=1)
    return jnp.asarray(bd, dtype=BF16)


def _position_dft_matrices(seq):
    lo = FOURIER_GROUP_DIM
    hi = seq // lo
    sp = jnp.arange(seq, dtype=jnp.int32)[None, :]
    a = jnp.arange(hi, dtype=jnp.int32)[:, None]
    b = jnp.arange(lo, dtype=jnp.int32)[:, None]
    w = 2.0 * math.pi / seq
    ang_a = ((a * lo * sp) % seq).astype(F32) * w
    ang_b = ((b * sp) % seq).astype(F32) * w
    ca, sa = jnp.cos(ang_a)[:, None, :], jnp.sin(ang_a)[:, None, :]
    cb, sb = jnp.cos(ang_b)[None, :, :], jnp.sin(ang_b)[None, :, :]
    cmat = (ca * cb - sa * sb).reshape(seq, seq).astype(BF16)
    smat = (-(sa * cb + ca * sb)).reshape(seq, seq).astype(BF16)
    return cmat, smat


def kernel(x_prompt, x_sample, norm_mix, w_in, lambda_qk, subln_gain, w_out, norm_ffn, ffn_w_gate_up,
           ffn_w_down, router_w, expert_w_gate_up, expert_w_down, final_norm):
    nb_p, seq_p, _ = x_prompt.shape
    nb_s, seq_s, _ = x_sample.shape
    n_p, n_s = nb_p * seq_p, nb_s * seq_s
    n = n_p + n_s
    tm = TOKEN_TILE
    assert seq_p % tm == 0 and seq_s % tm == 0
    assert n_p % seq_s == 0 and seq_p % ATTN_K_TILE == 0 and seq_s % ATTN_K_TILE == 0
    assert n_p % min(seq_s, DFT_K_TILE) == 0
    assert DEPTH == 2

    ct, sa, sb = _rope_tables(max(seq_p, seq_s))
    bd = _channel_dft_matrix()
    dft_p = _position_dft_matrices(seq_p)
    dft_s = dft_p if seq_s == seq_p else _position_dft_matrices(seq_s)

    tiles_p = n_p // tm
    two_arrays = _Split(tiles_p, 0)
    one_array = _Split(tiles_p, tiles_p)
    xa, xb, x_split = x_prompt.reshape(n_p, D_MODEL), x_sample.reshape(n_s, D_MODEL), two_arrays
    out = None
    for layer in range(DEPTH):
        lam_init = 0.8 - 0.6 * math.exp(-0.3 * layer)
        yf, qkv = _norm_inproj(xa, xb, x_split, n, norm_mix[layer][None, :], w_in[layer].astype(BF16),
                               bd, ct, sa, sb, seq_p, seq_s)
        gain_col = subln_gain[layer][:, None]
        o_a = (_attention(qkv, lambda_qk[layer], gain_col, 0, nb_p, seq_p, lam_init),
               _attention(qkv, lambda_qk[layer], gain_col, n_p, nb_s, seq_s, lam_init))
        o_f = (_fourier(yf, dft_p[0], dft_p[1], 0, nb_p, seq_p),
               _fourier(yf, dft_s[0], dft_s[1], n_p, nb_s, seq_s))
        x1, h = _outproj(xa, xb, x_split, o_f, o_a, two_arrays, n, w_out[layer].astype(BF16),
                         norm_ffn[layer][None, :])
        if layer % 2 == 0:
            x = _dense_ffn(x1, h, ffn_w_gate_up[layer // 2].astype(BF16), ffn_w_down[layer // 2].astype(BF16))
            xa, xb, x_split = x, x, one_array
        else:
            j = layer // 2
            rt, cnt = _router(x1, norm_ffn[layer][None, :], router_w[j].T)
            te = EXPERT_TILE
            counts = cnt[:, 0].astype(jnp.int32)
            padded = ((counts + te - 1) // te) * te
            ends = jnp.cumsum(padded)
            offs = ends - padded
            e1, e2 = rt[0].astype(jnp.int32), rt[1].astype(jnp.int32)
            slot1 = offs[e1] + rt[2].astype(jnp.int32)
            slot2 = offs[e2] + rt[3].astype(jnp.int32)
            n_slots = 2 * n + N_EXPERTS * te
            starts = jnp.arange(n_slots // te, dtype=jnp.int32) * te
            tile_expert = jnp.minimum(jnp.sum((starts[:, None] >= ends[None, :]).astype(jnp.int32), axis=1),
                                      N_EXPERTS - 1)
            tile_valid = jnp.clip(counts[tile_expert] - (starts - offs[tile_expert]), 0, te).astype(jnp.int32)
            n_tiles = (ends[-1:] // te).astype(jnp.int32)
            xs = _dispatch(slot1, slot2, h, n_slots)
            ys = _experts(tile_expert, n_tiles, tile_valid, xs,
                          expert_w_gate_up[j].astype(BF16), expert_w_down[j].astype(BF16))
            out = _combine(slot1, slot2, rt, x1, final_norm[None, :], ys, n_p)
    y_p, y_s = out
    return (y_p.reshape(nb_p, seq_p, D_MODEL), y_s.reshape(nb_s, seq_s, D_MODEL))
```

```python
import functools
import math

import numpy as np
import jax
import jax.numpy as jnp
from jax import lax
from jax.experimental import pallas as pl
from jax.experimental.pallas import tpu as pltpu

F32 = jnp.float32
BF16 = jnp.bfloat16

D_MODEL = 1024
DEPTH = 2
FOURIER_WIDTH = 256
FOURIER_GROUPS = 4
FOURIER_GROUP_DIM = 64
ATTN_WIDTH = 768
DIFF_HEAD_DIM = 64
DIFF_V_DIM = 128
N_DIFF_HEADS = 6
QK_WIDTH = 768
IN_PROJ_WIDTH = 2560
ROT_DIM = 16
ROPE_THETA = 500000.0
FF_DENSE = 2816
N_EXPERTS = 8
FF_EXPERT = 3584
EPS = 1e-5

LANES = 128
SUBLANES = 8
VMEM_LIMIT_BYTES = 56 * 1024 * 1024

TOKEN_TILE = 512
ATTN_Q_TILE = 512
ATTN_K_TILE = 512
ATTN_K_SUPER = 2048
LOG2E = 1.4426950408889634
DFT_ROW_TILE = 1024
DFT_K_TILE = 2048
FFN_CHUNK = 256
EXPERT_TILE = 1024
EXPERT_CHUNK = 1792
ROW_DMA_UNROLL = 8


def _cparams(sem):
    return pltpu.CompilerParams(dimension_semantics=sem, vmem_limit_bytes=VMEM_LIMIT_BYTES)


def _aligned(start, multiple):
    return start if isinstance(start, int) else pl.multiple_of(start, multiple)


def _silu_mul(g, u):
    return (g * (1.0 / (1.0 + jnp.exp(-g)))) * u


class _Split:
    def __init__(self, tiles_a, b_blk0):
        self.tiles_a, self.b_blk0 = tiles_a, b_blk0

    def specs(self, tm, width):
        ta, b0 = self.tiles_a, self.b_blk0
        return [pl.BlockSpec((tm, width), lambda i: (jnp.minimum(i, ta - 1), 0)),
                pl.BlockSpec((tm, width), lambda i: (b0 + jnp.maximum(i - ta, 0), 0))]

    def pick(self, a_ref, b_ref):
        return jnp.where(pl.program_id(0) < self.tiles_a, a_ref[...], b_ref[...])


def _norm_inproj_kernel(xa_ref, xb_ref, g_ref, w_ref, bd_ref, ct_ref, sa_ref, sb_ref, yf_ref, qkv_ref, h_scr,
                        *, split):
    x = split.pick(xa_ref, xb_ref)
    ms = jnp.mean(x * x, axis=-1, keepdims=True)
    h_scr[...] = ((x * lax.rsqrt(ms + EPS)) * g_ref[...]).astype(BF16)
    hb = h_scr[...]
    uf = jnp.dot(hb, w_ref[:, 0:FOURIER_WIDTH], preferred_element_type=F32)
    yf_ref[...] = jnp.dot(uf.astype(BF16), bd_ref[...], preferred_element_type=F32).astype(BF16)
    ct, sa, sb = ct_ref[...], sa_ref[...], sb_ref[...]
    width = 2 * LANES
    for blk in range(2 * QK_WIDTH // width):
        c0 = FOURIER_WIDTH + blk * width
        t = jnp.dot(hb, w_ref[:, c0:c0 + width], preferred_element_type=F32)
        r = t * ct + pltpu.roll(t, width - ROT_DIM // 2, 1) * sa + pltpu.roll(t, ROT_DIM // 2, 1) * sb
        if blk < QK_WIDTH // width:
            r = r * (DIFF_HEAD_DIM ** -0.5 * LOG2E)
        qkv_ref[:, blk * width:(blk + 1) * width] = r.astype(BF16)
    v0 = FOURIER_WIDTH + 2 * QK_WIDTH
    for blk in range(ATTN_WIDTH // width):
        t = jnp.dot(hb, w_ref[:, v0 + blk * width:v0 + (blk + 1) * width], preferred_element_type=F32)
        qkv_ref[:, 2 * QK_WIDTH + blk * width:2 * QK_WIDTH + (blk + 1) * width] = t.astype(BF16)


def _norm_inproj(xa, xb, split, n, g, w_bf, bd, ct, sa, sb, seq_p, seq_s):
    tm = TOKEN_TILE
    tiles_p = split.tiles_a
    per_p, per_s = seq_p // tm, seq_s // tm

    def pos_map(i):
        return (jnp.where(i < tiles_p, i % per_p, (i - tiles_p) % per_s), 0)

    tab = pl.BlockSpec((tm, 2 * LANES), pos_map)
    return pl.pallas_call(
        functools.partial(_norm_inproj_kernel, split=split),
        grid=(n // tm,),
        in_specs=split.specs(tm, D_MODEL) + [
            pl.BlockSpec((1, D_MODEL), lambda i: (0, 0)),
            pl.BlockSpec((D_MODEL, IN_PROJ_WIDTH), lambda i: (0, 0)),
            pl.BlockSpec((FOURIER_WIDTH, 2 * FOURIER_WIDTH), lambda i: (0, 0)),
            tab, tab, tab,
        ],
        out_specs=[
            pl.BlockSpec((tm, 2 * FOURIER_WIDTH), lambda i: (i, 0)),
            pl.BlockSpec((tm, 2 * QK_WIDTH + ATTN_WIDTH), lambda i: (i, 0)),
        ],
        out_shape=[
            jax.ShapeDtypeStruct((n, 2 * FOURIER_WIDTH), BF16),
            jax.ShapeDtypeStruct((n, 2 * QK_WIDTH + ATTN_WIDTH), BF16),
        ],
        scratch_shapes=[pltpu.VMEM((tm, D_MODEL), BF16)],
        compiler_params=_cparams(("arbitrary",)),
        name="norm_inproj",
    )(xa, xb, g, w_bf, bd, ct, sa, sb)


def _attn_kernel(lq_ref, q_ref, k_ref, v_ref, g_ref, o_ref, vt_scr, sa_scr, sb_scr, p_scr, acc_scr, *,
                 seq, lam_init):
    tq, tk = ATTN_Q_TILE, ATTN_K_TILE
    ks = min(seq, ATTN_K_SUPER)
    nsb, nkc = seq // ks, ks // tk
    groups = tk // SUBLANES
    n_units = (seq // tq) * nsb
    assert n_units % 2 == 0
    s_bufs = (sa_scr, sb_scr)
    for sb in range(nsb):
        for j in range(nkc):
            r0 = sb * ks + j * tk
            vt_scr[sb, :, j * tk:(j + 1) * tk] = v_ref[r0:r0 + tk, :].astype(F32).T.astype(BF16)

    lq = lq_ref[...]
    lam = (jnp.exp(jnp.sum(lq[0:1, :] * lq[1:2, :], axis=1, keepdims=True))
           - jnp.exp(jnp.sum(lq[2:3, :] * lq[3:4, :], axis=1, keepdims=True)) + lam_init)
    gain = g_ref[...]
    first = lax.broadcasted_iota(jnp.int32, (1, DIFF_V_DIM), 1) < DIFF_HEAD_DIM
    nt_dims = (((1,), (1,)), ((), ()))
    comps = (0, 1)

    def load_q(u):
        q = q_ref[pl.ds(_aligned((u // nsb) * tq, tq), tq), :]
        return (jnp.where(first, q, jnp.zeros_like(q)), jnp.where(first, jnp.zeros_like(q), q))

    def score_chunk(u, qc, s_scr, j, mx):
        kblk = k_ref[pl.ds(_aligned((u % nsb) * ks + j * tk, tk), tk), :]
        out = []
        for c in comps:
            s = lax.dot_general(kblk, qc[c], nt_dims, preferred_element_type=F32)
            s_scr[c, j] = s
            out.append(jnp.maximum(mx[c], jnp.max(s.reshape(groups, SUBLANES, tq), axis=0)))
        return out

    def exp_chunk(s_scr, j, m_new, ls):
        out = []
        for c in comps:
            p = jnp.exp2(s_scr[c, j] - m_new[c])
            out.append(ls[c] + jnp.sum(p.reshape(groups, SUBLANES, tq), axis=0))
            p_scr[c, j * tk:(j + 1) * tk, :] = p.astype(BF16)
        return out

    def unit(u, slot, st, with_next):
        mx, m, l = st[0:2], st[2:4], st[4:6]
        sb, qi = u % nsb, u // nsb
        m_new = [jnp.maximum(m[c], jnp.max(mx[c], axis=0, keepdims=True)) for c in comps]
        alpha = [jnp.exp2(m[c] - m_new[c]) for c in comps]
        ls = [jnp.zeros((SUBLANES, tq), F32) for _ in comps]
        mx_next = [jnp.full((SUBLANES, tq), -jnp.inf, F32) for _ in comps]
        qc_next = load_q(u + 1) if with_next else None
        for j in range(nkc):
            if with_next:
                mx_next = score_chunk(u + 1, qc_next, s_bufs[1 - slot], j, mx_next)
            ls = exp_chunk(s_bufs[slot], j, m_new, ls)
        l_new = [alpha[c] * l[c] + jnp.sum(ls[c], axis=0, keepdims=True) for c in comps]
        vt = vt_scr[sb]
        for c in comps:
            acc_scr[c] = alpha[c] * acc_scr[c] + jnp.dot(vt, p_scr[c], preferred_element_type=F32)

        last = jnp.asarray(sb == nsb - 1)

        @pl.when(last)
        def _():
            o_t = acc_scr[0] * (1.0 / l_new[0]) - lam * (acc_scr[1] * (1.0 / l_new[1]))
            ms = jnp.mean(o_t * o_t, axis=0, keepdims=True)
            y_t = ((o_t * lax.rsqrt(ms + EPS)) * gain) * (1.0 - lam_init)
            o_ref[pl.ds(_aligned(qi * tq, tq), tq), :] = y_t.T.astype(BF16)
            acc_scr[...] = jnp.zeros_like(acc_scr)

        m_out = [jnp.where(last, -jnp.inf, m_new[c]) for c in comps]
        l_out = [jnp.where(last, 0.0, l_new[c]) for c in comps]
        return (mx_next[0], mx_next[1], m_out[0], m_out[1], l_out[0], l_out[1])

    acc_scr[...] = jnp.zeros_like(acc_scr)
    mx0 = [jnp.full((SUBLANES, tq), -jnp.inf, F32) for _ in comps]
    qc0 = load_q(0)
    for j in range(nkc):
        mx0 = score_chunk(0, qc0, s_bufs[0], j, mx0)
    neg = jnp.full((1, tq), -jnp.inf, F32)
    zero = jnp.zeros((1, tq), F32)
    st = (mx0[0], mx0[1], neg, neg, zero, zero)

    def pair(i, st):
        st = unit(2 * i, 0, st, True)
        return unit(2 * i + 1, 1, st, True)

    st = lax.fori_loop(0, n_units // 2 - 1, pair, st)
    st = unit(n_units - 2, 0, st, True)
    unit(n_units - 1, 1, st, False)


def _attention(qkv, lq, gain_col, row0, nb, seq, lam_init):
    blk0 = row0 // seq
    nh = N_DIFF_HEADS
    tq, tk = ATTN_Q_TILE, ATTN_K_TILE
    ks = min(seq, ATTN_K_SUPER)
    kern = functools.partial(_attn_kernel, seq=seq, lam_init=lam_init)
    return pl.pallas_call(
        kern,
        grid=(nb, nh),
        in_specs=[
            pl.BlockSpec((4, DIFF_HEAD_DIM), lambda b, h: (0, 0)),
            pl.BlockSpec((seq, LANES), lambda b, h: (blk0 + b, h)),
            pl.BlockSpec((seq, LANES), lambda b, h: (blk0 + b, nh + h)),
            pl.BlockSpec((seq, LANES), lambda b, h: (blk0 + b, 2 * nh + h)),
            pl.BlockSpec((DIFF_V_DIM, 1), lambda b, h: (0, 0)),
        ],
        out_specs=pl.BlockSpec((seq, LANES), lambda b, h: (b, h)),
        out_shape=jax.ShapeDtypeStruct((nb * seq, ATTN_WIDTH), BF16),
        scratch_shapes=[
            pltpu.VMEM((seq // ks, DIFF_V_DIM, ks), BF16),
            pltpu.VMEM((2, ks // tk, tk, tq), F32),
            pltpu.VMEM((2, ks // tk, tk, tq), F32),
            pltpu.VMEM((2, ks, tq), BF16),
            pltpu.VMEM((2, DIFF_V_DIM, tq), F32),
        ],
        compiler_params=_cparams(("arbitrary", "arbitrary")),
        name="diff_attention",
    )(lq, qkv, qkv, qkv, gain_col)


def _dft_kernel(c_ref, s_ref, y_ref, o_ref, acc_scr, *, nk, scale):
    k = pl.program_id(2)
    y = y_ref[...]
    part = (jnp.dot(c_ref[...], y[:, 0:FOURIER_WIDTH], preferred_element_type=F32)
            + jnp.dot(s_ref[...], y[:, FOURIER_WIDTH:2 * FOURIER_WIDTH], preferred_element_type=F32))

    @pl.when(k == 0)
    def _():
        acc_scr[...] = part

    @pl.when(k > 0)
    def _():
        acc_scr[...] += part

    @pl.when(k == nk - 1)
    def _():
        o_ref[...] = (acc_scr[...] * scale).astype(BF16)


def _fourier(yf, cmat, smat, row0, nb, seq):
    tm = min(seq, DFT_ROW_TILE)
    tk = min(seq, DFT_K_TILE)
    ni, nk = seq // tm, seq // tk
    yblk0 = row0 // tk
    kern = functools.partial(_dft_kernel, nk=nk, scale=1.0 / math.sqrt(seq))
    return pl.pallas_call(
        kern,
        grid=(ni, nb, nk),
        in_specs=[
            pl.BlockSpec((tm, tk), lambda i, b, k: (i, k)),
            pl.BlockSpec((tm, tk), lambda i, b, k: (i, k)),
            pl.BlockSpec((tk, 2 * FOURIER_WIDTH), lambda i, b, k: (yblk0 + b * nk + k, 0)),
        ],
        out_specs=pl.BlockSpec((tm, FOURIER_WIDTH), lambda i, b, k: (b * ni + i, 0)),
        out_shape=jax.ShapeDtypeStruct((nb * seq, FOURIER_WIDTH), BF16),
        scratch_shapes=[pltpu.VMEM((tm, FOURIER_WIDTH), F32)],
        compiler_params=_cparams(("arbitrary", "arbitrary", "arbitrary")),
        name="fourier_dft",
    )(cmat, smat, yf)


def _outproj_kernel(xa_ref, xb_ref, ofa_ref, ofb_ref, oaa_ref, oab_ref, w_ref, g_ref, *rest, x_split, o_split,
                    with_router):
    if with_router:
        rw_ref, x1_ref, h_ref, rt_ref, cnt_ref, carry_scr = rest
    else:
        x1_ref, h_ref = rest
    y = (jnp.dot(o_split.pick(ofa_ref, ofb_ref), w_ref[0:FOURIER_WIDTH, :], preferred_element_type=F32)
         + jnp.dot(o_split.pick(oaa_ref, oab_ref), w_ref[FOURIER_WIDTH:D_MODEL, :], preferred_element_type=F32))
    x1 = x_split.pick(xa_ref, xb_ref) + y
    x1_ref[...] = x1
    ms = jnp.mean(x1 * x1, axis=-1, keepdims=True)
    h = (x1 * lax.rsqrt(ms + EPS)) * g_ref[...]
    h_ref[...] = h.astype(BF16)
    if with_router:
        _route(h, rw_ref, rt_ref, cnt_ref, carry_scr)


def _outproj(xa, xb, x_split, o_f, o_a, o_split, n, w_bf, g, rw_t=None):
    tm = TOKEN_TILE
    with_router = rw_t is not None
    in_specs = (x_split.specs(tm, D_MODEL) + o_split.specs(tm, FOURIER_WIDTH) + o_split.specs(tm, ATTN_WIDTH) + [
        pl.BlockSpec((D_MODEL, D_MODEL), lambda i: (0, 0)),
        pl.BlockSpec((1, D_MODEL), lambda i: (0, 0)),
    ])
    out_specs = [pl.BlockSpec((tm, D_MODEL), lambda i: (i, 0)), pl.BlockSpec((tm, D_MODEL), lambda i: (i, 0))]
    out_shape = [jax.ShapeDtypeStruct((n, D_MODEL), F32), jax.ShapeDtypeStruct((n, D_MODEL), BF16)]
    operands = [xa, xb, o_f[0], o_f[1], o_a[0], o_a[1], w_bf, g]
    scratch = []
    if with_router:
        in_specs.append(pl.BlockSpec((N_EXPERTS, D_MODEL), lambda i: (0, 0)))
        out_specs += [pl.BlockSpec((SUBLANES, tm), lambda i: (0, i)), pl.BlockSpec((N_EXPERTS, LANES), lambda i: (0, 0))]
        out_shape += [jax.ShapeDtypeStruct((SUBLANES, n), F32), jax.ShapeDtypeStruct((N_EXPERTS, LANES), F32)]
        operands.append(rw_t)
        scratch.append(pltpu.VMEM((N_EXPERTS, LANES), F32))
    return pl.pallas_call(
        functools.partial(_outproj_kernel, x_split=x_split, o_split=o_split, with_router=with_router),
        grid=(n // tm,),
        in_specs=in_specs,
        out_specs=out_specs,
        out_shape=out_shape,
        scratch_shapes=scratch,
        compiler_params=_cparams(("arbitrary",)),
        name="outproj_norm",
    )(*operands)


def _ffn_kernel(x1_ref, h_ref, wgu_ref, wd_ref, o_ref, acc_scr):
    hb = h_ref[...]
    fc = FFN_CHUNK
    for j in range(FF_DENSE // fc):
        g = jnp.dot(hb, wgu_ref[:, j * fc:(j + 1) * fc], preferred_element_type=F32)
        u = jnp.dot(hb, wgu_ref[:, FF_DENSE + j * fc:FF_DENSE + (j + 1) * fc], preferred_element_type=F32)
        c = jnp.dot(_silu_mul(g, u).astype(BF16), wd_ref[j * fc:(j + 1) * fc, :], preferred_element_type=F32)
        if j == 0:
            acc_scr[...] = c
        else:
            acc_scr[...] += c
    o_ref[...] = x1_ref[...] + acc_scr[...]


def _dense_ffn(x1, h, wgu_bf, wd_bf):
    n = x1.shape[0]
    tm = TOKEN_TILE
    return pl.pallas_call(
        _ffn_kernel,
        grid=(n // tm,),
        in_specs=[
            pl.BlockSpec((tm, D_MODEL), lambda i: (i, 0)),
            pl.BlockSpec((tm, D_MODEL), lambda i: (i, 0)),
            pl.BlockSpec((D_MODEL, 2 * FF_DENSE), lambda i: (0, 0)),
            pl.BlockSpec((FF_DENSE, D_MODEL), lambda i: (0, 0)),
        ],
        out_specs=pl.BlockSpec((tm, D_MODEL), lambda i: (i, 0)),
        out_shape=jax.ShapeDtypeStruct((n, D_MODEL), F32),
        scratch_shapes=[pltpu.VMEM((tm, D_MODEL), F32)],
        compiler_params=_cparams(("arbitrary",)),
        name="dense_swiglu",
    )(x1, h, wgu_bf, wd_bf)


def _route(h, rw_ref, rt_ref, cnt_ref, carry_scr):
    tm = TOKEN_TILE

    @pl.when(pl.program_id(0) == 0)
    def _():
        carry_scr[...] = jnp.zeros_like(carry_scr)

    h_hi = h.astype(BF16)
    h_lo = (h - h_hi.astype(F32)).astype(BF16)
    rw = rw_ref[...]
    rw_hi = rw.astype(BF16)
    rw_lo = (rw - rw_hi.astype(F32)).astype(BF16)
    nt_dims = (((1,), (1,)), ((), ()))
    dg = lambda a, b: lax.dot_general(a, b, nt_dims, preferred_element_type=F32)
    logits = dg(rw_hi, h_hi) + (dg(rw_hi, h_lo) + dg(rw_lo, h_hi))

    eidx = lax.broadcasted_iota(jnp.int32, (N_EXPERTS, tm), 0)
    m1 = jnp.max(logits, axis=0, keepdims=True)
    i1 = jnp.min(jnp.where(logits == m1, eidx, N_EXPERTS), axis=0, keepdims=True)
    oh1 = eidx == i1
    rest = jnp.where(oh1, -jnp.inf, logits)
    m2 = jnp.max(rest, axis=0, keepdims=True)
    i2 = jnp.min(jnp.where(rest == m2, eidx, N_EXPERTS), axis=0, keepdims=True)
    oh2 = eidx == i2
    e = jnp.exp(m2 - m1)
    g1 = 1.0 / (1.0 + e)
    g2 = e / (1.0 + e)

    onehot = jnp.where(oh1 | oh2, 1.0, 0.0)
    upper = (lax.broadcasted_iota(jnp.int32, (tm, tm), 0)
             < lax.broadcasted_iota(jnp.int32, (tm, tm), 1))
    before = jnp.dot(onehot.astype(BF16), jnp.where(upper, 1.0, 0.0).astype(BF16),
                     preferred_element_type=F32)
    rank = before + carry_scr[:, 0:1]
    r1 = jnp.sum(jnp.where(oh1, rank, 0.0), axis=0, keepdims=True)
    r2 = jnp.sum(jnp.where(oh2, rank, 0.0), axis=0, keepdims=True)
    zero = jnp.zeros_like(g1)
    rt_ref[...] = jnp.concatenate(
        [i1.astype(F32), i2.astype(F32), r1, r2, g1, g2, zero, zero], axis=0)
    total = carry_scr[...] + jnp.sum(onehot, axis=1, keepdims=True)
    carry_scr[...] = total
    cnt_ref[...] = total


def _row_copy(src, src_row, dst, dst_row, sem):
    return pltpu.make_async_copy(src.at[pl.ds(src_row, 1)], dst.at[pl.ds(dst_row, 1)], sem)


def _dispatch_kernel(s1_ref, s2_ref, h_ref, xs_ref, rows_scr, sem):
    tm = TOKEN_TILE
    rows_scr[...] = h_ref[...].astype(F32)

    def issue(t, c):
        _row_copy(rows_scr, t, xs_ref, s1_ref[t], sem).start()
        _row_copy(rows_scr, t, xs_ref, s2_ref[t], sem).start()
        return c

    lax.fori_loop(0, tm, issue, 0, unroll=ROW_DMA_UNROLL)
    for _ in range(2):
        pltpu.make_async_copy(rows_scr, xs_ref.at[pl.ds(0, tm)], sem).wait()


def _dispatch(slot1, slot2, h, n_slots):
    n = h.shape[0]
    tm = TOKEN_TILE
    smem = lambda: pl.BlockSpec((tm,), lambda i: (i,), memory_space=pltpu.SMEM)
    return pl.pallas_call(
        _dispatch_kernel,
        grid=(n // tm,),
        in_specs=[smem(), smem(), pl.BlockSpec((tm, D_MODEL), lambda i: (i, 0))],
        out_specs=pl.BlockSpec(memory_space=pl.ANY),
        out_shape=jax.ShapeDtypeStruct((n_slots, D_MODEL), F32),
        scratch_shapes=[pltpu.VMEM((tm, D_MODEL), F32), pltpu.SemaphoreType.DMA(())],
        compiler_params=_cparams(("arbitrary",)),
        name="moe_dispatch",
    )(slot1, slot2, h)


def _expert_kernel(te_ref, nt_ref, tv_ref, xs_ref, wg_ref, wu_ref, wd_ref, ys_ref, xb_scr):
    i = pl.program_id(0)
    j = pl.program_id(1)

    @pl.when(i < nt_ref[0])
    def _():
        @pl.when(j == 0)
        def _():
            rows = lax.broadcasted_iota(jnp.int32, (EXPERT_TILE, 1), 0)
            xb_scr[...] = jnp.where(rows < tv_ref[i], xs_ref[...], 0.0).astype(BF16)
            ys_ref[...] = jnp.zeros_like(ys_ref)

        xb = xb_scr[...]
        fc = FFN_CHUNK
        for t in range(EXPERT_CHUNK // fc):
            g = jnp.dot(xb, wg_ref[0, :, t * fc:(t + 1) * fc], preferred_element_type=F32)
            u = jnp.dot(xb, wu_ref[0, :, t * fc:(t + 1) * fc], preferred_element_type=F32)
            ys_ref[...] += jnp.dot(_silu_mul(g, u).astype(BF16), wd_ref[0, t * fc:(t + 1) * fc, :],
                                   preferred_element_type=F32)


def _experts(tile_expert, n_tiles, tile_valid, xs, wgu_bf, wd_bf):
    n_slots = xs.shape[0]
    te, fc = EXPERT_TILE, EXPERT_CHUNK
    nj = FF_EXPERT // fc
    max_tiles = n_slots // te

    def tile(i, nt):
        return jnp.minimum(i, nt[0] - 1)

    def chunk(i, j, nt):
        return jnp.where(i < nt[0], j, nj - 1)

    grid_spec = pltpu.PrefetchScalarGridSpec(
        num_scalar_prefetch=3,
        grid=(max_tiles, nj),
        in_specs=[
            pl.BlockSpec((te, D_MODEL), lambda i, j, e, nt, tv: (tile(i, nt), 0)),
            pl.BlockSpec((1, D_MODEL, fc), lambda i, j, e, nt, tv: (e[tile(i, nt)], 0, chunk(i, j, nt))),
            pl.BlockSpec((1, D_MODEL, fc), lambda i, j, e, nt, tv: (e[tile(i, nt)], 0, nj + chunk(i, j, nt))),
            pl.BlockSpec((1, fc, D_MODEL), lambda i, j, e, nt, tv: (e[tile(i, nt)], chunk(i, j, nt), 0)),
        ],
        out_specs=pl.BlockSpec((te, D_MODEL), lambda i, j, e, nt, tv: (tile(i, nt), 0)),
        scratch_shapes=[pltpu.VMEM((te, D_MODEL), BF16)],
    )
    return pl.pallas_call(
        _expert_kernel,
        grid_spec=grid_spec,
        out_shape=jax.ShapeDtypeStruct((n_slots, D_MODEL), F32),
        compiler_params=_cparams(("arbitrary", "arbitrary")),
        name="expert_swiglu",
    )(tile_expert, n_tiles, tile_valid, xs, wgu_bf, wgu_bf, wd_bf)


def _combine_kernel(s1_ref, s2_ref, rt_ref, x1_ref, g_ref, ys_ref, op_ref, os_ref, a_scr, b_scr, sem, *,
                    tiles_p):
    tm = TOKEN_TILE
    i = pl.program_id(0)

    def issue(t, c):
        _row_copy(ys_ref, s1_ref[t], a_scr, t, sem).start()
        _row_copy(ys_ref, s2_ref[t], b_scr, t, sem).start()
        return c

    lax.fori_loop(0, tm, issue, 0, unroll=ROW_DMA_UNROLL)

    rt = jnp.concatenate([rt_ref[...], jnp.zeros((LANES - SUBLANES, tm), F32)], axis=0)
    rt_t = rt.T
    g1 = rt_t[:, 4:5]
    g2 = rt_t[:, 5:6]

    pltpu.make_async_copy(ys_ref.at[pl.ds(0, tm)], a_scr, sem).wait()
    pltpu.make_async_copy(ys_ref.at[pl.ds(0, tm)], b_scr, sem).wait()

    x2 = x1_ref[...] + (g1 * a_scr[...] + g2 * b_scr[...])
    ms = jnp.mean(x2 * x2, axis=-1, keepdims=True)
    y = (x2 * lax.rsqrt(ms + EPS)) * g_ref[...]

    @pl.when(i < tiles_p)
    def _():
        op_ref[...] = y

    @pl.when(i >= tiles_p)
    def _():
        os_ref[...] = y


def _combine(slot1, slot2, rt, x1, g_final, ys, n_p):
    n = x1.shape[0]
    tm = TOKEN_TILE
    tiles_p = n_p // tm
    smem = lambda: pl.BlockSpec((tm,), lambda i: (i,), memory_space=pltpu.SMEM)
    return pl.pallas_call(
        functools.partial(_combine_kernel, tiles_p=tiles_p),
        grid=(n // tm,),
        in_specs=[
            smem(), smem(),
            pl.BlockSpec((SUBLANES, tm), lambda i: (0, i)),
            pl.BlockSpec((tm, D_MODEL), lambda i: (i, 0)),
            pl.BlockSpec((1, D_MODEL), lambda i: (0, 0)),
            pl.BlockSpec(memory_space=pl.ANY),
        ],
        out_specs=[
            pl.BlockSpec((tm, D_MODEL), lambda i: (jnp.minimum(i, tiles_p - 1), 0)),
            pl.BlockSpec((tm, D_MODEL), lambda i: (jnp.maximum(i - tiles_p, 0), 0)),
        ],
        out_shape=[
            jax.ShapeDtypeStruct((n_p, D_MODEL), F32),
            jax.ShapeDtypeStruct((n - n_p, D_MODEL), F32),
        ],
        scratch_shapes=[
            pltpu.VMEM((tm, D_MODEL), F32),
            pltpu.VMEM((tm, D_MODEL), F32),
            pltpu.SemaphoreType.DMA(()),
        ],
        compiler_params=_cparams(("arbitrary",)),
        name="moe_combine_norm",
    )(slot1, slot2, rt, x1, g_final, ys)


def _rope_tables(seq_max):
    half = ROT_DIM // 2
    inv = ROPE_THETA ** (-jnp.arange(0, ROT_DIM, 2, dtype=F32) / ROT_DIM)
    ang = jnp.arange(seq_max, dtype=F32)[:, None] * inv[None, :]
    cos, sin = jnp.cos(ang), jnp.sin(ang)
    ones = jnp.ones((seq_max, DIFF_HEAD_DIM - ROT_DIM), F32)
    zeros = jnp.zeros((seq_max, DIFF_HEAD_DIM - ROT_DIM), F32)
    z8 = jnp.zeros((seq_max, half), F32)
    ct = jnp.concatenate([cos, cos, ones], axis=1)
    sa = jnp.concatenate([-sin, z8, zeros], axis=1)
    sb = jnp.concatenate([z8, sin, zeros], axis=1)
    reps = 2 * LANES // DIFF_HEAD_DIM
    return tuple(jnp.tile(t, (1, reps)) for t in (ct, sa, sb))


def _channel_dft_matrix():
    c = np.arange(FOURIER_GROUP_DIM)
    ang = 2.0 * np.pi * ((c[:, None] * c[None, :]) % FOURIER_GROUP_DIM) / FOURIER_GROUP_DIM
    scale = FOURIER_GROUP_DIM ** -0.5
    eye = np.eye(FOURIER_GROUPS)
    bd = np.concatenate([np.kron(eye, np.cos(ang) * scale), np.kron(eye, np.sin(ang) * scale)], axis=1)
    return jnp.asarray(bd, dtype=BF16)


def _position_dft_matrices(seq):
    lo = FOURIER_GROUP_DIM
    hi = seq // lo
    sp = jnp.arange(seq, dtype=jnp.int32)[None, :]
    a = jnp.arange(hi, dtype=jnp.int32)[:, None]
    b = jnp.arange(lo, dtype=jnp.int32)[:, None]
    w = 2.0 * math.pi / seq
    ang_a = ((a * lo * sp) % seq).astype(F32) * w
    ang_b = ((b * sp) % seq).astype(F32) * w
    ca, sa = jnp.cos(ang_a)[:, None, :], jnp.sin(ang_a)[:, None, :]
    cb, sb = jnp.cos(ang_b)[None, :, :], jnp.sin(ang_b)[None, :, :]
    cmat = (ca * cb - sa * sb).reshape(seq, seq).astype(BF16)
    smat = (-(sa * cb + ca * sb)).reshape(seq, seq).astype(BF16)
    return cmat, smat


def kernel(x_prompt, x_sample, norm_mix, w_in, lambda_qk, subln_gain, w_out, norm_ffn, ffn_w_gate_up,
           ffn_w_down, router_w, expert_w_gate_up, expert_w_down, final_norm):
    nb_p, seq_p, _ = x_prompt.shape
    nb_s, seq_s, _ = x_sample.shape
    n_p, n_s = nb_p * seq_p, nb_s * seq_s
    n = n_p + n_s
    tm = TOKEN_TILE
    assert seq_p % tm == 0 and seq_s % tm == 0
    assert n_p % seq_s == 0 and seq_p % ATTN_K_TILE == 0 and seq_s % ATTN_K_TILE == 0
    assert n_p % min(seq_s, DFT_K_TILE) == 0
    assert DEPTH == 2

    ct, sa, sb = _rope_tables(max(seq_p, seq_s))
    bd = _channel_dft_matrix()
    dft_p = _position_dft_matrices(seq_p)
    dft_s = dft_p if seq_s == seq_p else _position_dft_matrices(seq_s)

    tiles_p = n_p // tm
    two_arrays = _Split(tiles_p, 0)
    one_array = _Split(tiles_p, tiles_p)
    xa, xb, x_split = x_prompt.reshape(n_p, D_MODEL), x_sample.reshape(n_s, D_MODEL), two_arrays
    out = None
    for layer in range(DEPTH):
        lam_init = 0.8 - 0.6 * math.exp(-0.3 * layer)
        yf, qkv = _norm_inproj(xa, xb, x_split, n, norm_mix[layer][None, :], w_in[layer].astype(BF16),
                               bd, ct, sa, sb, seq_p, seq_s)
        gain_col = subln_gain[layer][:, None]
        o_a = (_attention(qkv, lambda_qk[layer], gain_col, 0, nb_p, seq_p, lam_init),
               _attention(qkv, lambda_qk[layer], gain_col, n_p, nb_s, seq_s, lam_init))
        o_f = (_fourier(yf, dft_p[0], dft_p[1], 0, nb_p, seq_p),
               _fourier(yf, dft_s[0], dft_s[1], n_p, nb_s, seq_s))
        if layer % 2 == 0:
            x1, h = _outproj(xa, xb, x_split, o_f, o_a, two_arrays, n, w_out[layer].astype(BF16),
                             norm_ffn[layer][None, :])
            x = _dense_ffn(x1, h, ffn_w_gate_up[layer // 2].astype(BF16), ffn_w_down[layer // 2].astype(BF16))
            xa, xb, x_split = x, x, one_array
        else:
            j = layer // 2
            x1, h, rt, cnt = _outproj(xa, xb, x_split, o_f, o_a, two_arrays, n, w_out[layer].astype(BF16),
                                      norm_ffn[layer][None, :], rw_t=router_w[j].T)
            te = EXPERT_TILE
            counts = cnt[:, 0].astype(jnp.int32)
            padded = ((counts + te - 1) // te) * te
            ends = jnp.cumsum(padded)
            offs = ends - padded
            e1, e2 = rt[0].astype(jnp.int32), rt[1].astype(jnp.int32)
            slot1 = offs[e1] + rt[2].astype(jnp.int32)
            slot2 = offs[e2] + rt[3].astype(jnp.int32)
            n_slots = 2 * n + N_EXPERTS * te
            starts = jnp.arange(n_slots // te, dtype=jnp.int32) * te
            tile_expert = jnp.minimum(jnp.sum((starts[:, None] >= ends[None, :]).astype(jnp.int32), axis=1),
                                      N_EXPERTS - 1)
            tile_valid = jnp.clip(counts[tile_expert] - (starts - offs[tile_expert]), 0, te).astype(jnp.int32)
            n_tiles = (ends[-1:] // te).astype(jnp.int32)
            xs = _dispatch(slot1, slot2, h, n_slots)
            ys = _experts(tile_expert, n_tiles, tile_valid, xs,
                          expert_w_gate_up[j].astype(BF16), expert_w_down[j].astype(BF16))
            out = _combine(slot1, slot2, rt, x1, final_norm[None, :], ys, n_p)
    y_p, y_s = out
    return (y_p.reshape(nb_p, seq_p, D_MODEL), y_s.reshape(nb_s, seq_s, D_MODEL))
```

```python
import functools
import math

import numpy as np
import jax
import jax.numpy as jnp
from jax import lax
from jax.experimental import pallas as pl
from jax.experimental.pallas import tpu as pltpu

F32 = jnp.float32
BF16 = jnp.bfloat16

D_MODEL = 1024
DEPTH = 2
FOURIER_WIDTH = 256
FOURIER_GROUPS = 4
FOURIER_GROUP_DIM = 64
ATTN_WIDTH = 768
DIFF_HEAD_DIM = 64
DIFF_V_DIM = 128
N_DIFF_HEADS = 6
QK_WIDTH = 768
IN_PROJ_WIDTH = 2560
ROT_DIM = 16
ROPE_THETA = 500000.0
FF_DENSE = 2816
N_EXPERTS = 8
FF_EXPERT = 3584
EPS = 1e-5

LANES = 128
SUBLANES = 8
VMEM_LIMIT_BYTES = 56 * 1024 * 1024

TOKEN_TILE = 512
ATTN_Q_TILE = 512
ATTN_K_TILE = 512
ATTN_K_SUPER = 2048
LOG2E = 1.4426950408889634
DFT_ROW_TILE = 1024
DFT_K_TILE = 2048
FFN_CHUNK = 256
EXPERT_TILE = 1024
EXPERT_CHUNK = 1792
ROW_DMA_UNROLL = 8


def _cparams(sem):
    return pltpu.CompilerParams(dimension_semantics=sem, vmem_limit_bytes=VMEM_LIMIT_BYTES)


def _aligned(start, multiple):
    return start if isinstance(start, int) else pl.multiple_of(start, multiple)


def _silu_mul(g, u):
    return (g * (1.0 / (1.0 + jnp.exp(-g)))) * u


class _Split:
    def __init__(self, tiles_a, b_blk0):
        self.tiles_a, self.b_blk0 = tiles_a, b_blk0

    def specs(self, tm, width):
        ta, b0 = self.tiles_a, self.b_blk0
        return [pl.BlockSpec((tm, width), lambda i: (jnp.minimum(i, ta - 1), 0)),
                pl.BlockSpec((tm, width), lambda i: (b0 + jnp.maximum(i - ta, 0), 0))]

    def pick(self, a_ref, b_ref):
        return jnp.where(pl.program_id(0) < self.tiles_a, a_ref[...], b_ref[...])


def _norm_inproj_kernel(xa_ref, xb_ref, g_ref, w_ref, bd_ref, ct_ref, sa_ref, sb_ref, yf_ref, qkv_ref, h_scr,
                        *, split):
    x = split.pick(xa_ref, xb_ref)
    ms = jnp.mean(x * x, axis=-1, keepdims=True)
    h_scr[...] = ((x * lax.rsqrt(ms + EPS)) * g_ref[...]).astype(BF16)
    hb = h_scr[...]
    uf = jnp.dot(hb, w_ref[:, 0:FOURIER_WIDTH], preferred_element_type=F32)
    yf_ref[...] = jnp.dot(uf.astype(BF16), bd_ref[...], preferred_element_type=F32).astype(BF16)
    ct, sa, sb = ct_ref[...], sa_ref[...], sb_ref[...]
    width = 2 * LANES
    for blk in range(2 * QK_WIDTH // width):
        c0 = FOURIER_WIDTH + blk * width
        t = jnp.dot(hb, w_ref[:, c0:c0 + width], preferred_element_type=F32)
        r = t * ct + pltpu.roll(t, width - ROT_DIM // 2, 1) * sa + pltpu.roll(t, ROT_DIM // 2, 1) * sb
        if blk < QK_WIDTH // width:
            r = r * (DIFF_HEAD_DIM ** -0.5 * LOG2E)
        qkv_ref[:, blk * width:(blk + 1) * width] = r.astype(BF16)
    v0 = FOURIER_WIDTH + 2 * QK_WIDTH
    for blk in range(ATTN_WIDTH // width):
        t = jnp.dot(hb, w_ref[:, v0 + blk * width:v0 + (blk + 1) * width], preferred_element_type=F32)
        qkv_ref[:, 2 * QK_WIDTH + blk * width:2 * QK_WIDTH + (blk + 1) * width] = t.astype(BF16)


def _norm_inproj(xa, xb, split, n, g, w_bf, bd, ct, sa, sb, seq_p, seq_s):
    tm = TOKEN_TILE
    tiles_p = split.tiles_a
    per_p, per_s = seq_p // tm, seq_s // tm

    def pos_map(i):
        return (jnp.where(i < tiles_p, i % per_p, (i - tiles_p) % per_s), 0)

    tab = pl.BlockSpec((tm, 2 * LANES), pos_map)
    return pl.pallas_call(
        functools.partial(_norm_inproj_kernel, split=split),
        grid=(n // tm,),
        in_specs=split.specs(tm, D_MODEL) + [
            pl.BlockSpec((1, D_MODEL), lambda i: (0, 0)),
            pl.BlockSpec((D_MODEL, IN_PROJ_WIDTH), lambda i: (0, 0)),
            pl.BlockSpec((FOURIER_WIDTH, 2 * FOURIER_WIDTH), lambda i: (0, 0)),
            tab, tab, tab,
        ],
        out_specs=[
            pl.BlockSpec((tm, 2 * FOURIER_WIDTH), lambda i: (i, 0)),
            pl.BlockSpec((tm, 2 * QK_WIDTH + ATTN_WIDTH), lambda i: (i, 0)),
        ],
        out_shape=[
            jax.ShapeDtypeStruct((n, 2 * FOURIER_WIDTH), BF16),
            jax.ShapeDtypeStruct((n, 2 * QK_WIDTH + ATTN_WIDTH), BF16),
        ],
        scratch_shapes=[pltpu.VMEM((tm, D_MODEL), BF16)],
        compiler_params=_cparams(("arbitrary",)),
        name="norm_inproj",
    )(xa, xb, g, w_bf, bd, ct, sa, sb)


def _attn_kernel(lq_ref, q_ref, k_ref, v_ref, g_ref, o_ref, vt_scr, sa_scr, sb_scr, p_scr, acc_scr, *,
                 seq, lam_init):
    tq, tk = ATTN_Q_TILE, ATTN_K_TILE
    ks = min(seq, ATTN_K_SUPER)
    nsb, nkc = seq // ks, ks // tk
    groups = tk // SUBLANES
    n_units = (seq // tq) * nsb
    assert n_units % 2 == 0
    s_bufs = (sa_scr, sb_scr)
    for sb in range(nsb):
        for j in range(nkc):
            r0 = sb * ks + j * tk
            vt_scr[sb, :, j * tk:(j + 1) * tk] = v_ref[r0:r0 + tk, :].astype(F32).T.astype(BF16)

    lq = lq_ref[...]
    lam = (jnp.exp(jnp.sum(lq[0:1, :] * lq[1:2, :], axis=1, keepdims=True))
           - jnp.exp(jnp.sum(lq[2:3, :] * lq[3:4, :], axis=1, keepdims=True)) + lam_init)
    gain = g_ref[...]
    first = lax.broadcasted_iota(jnp.int32, (1, DIFF_V_DIM), 1) < DIFF_HEAD_DIM
    nt_dims = (((1,), (1,)), ((), ()))
    comps = (0, 1)

    def load_q(u):
        q = q_ref[pl.ds(_aligned((u // nsb) * tq, tq), tq), :]
        return (jnp.where(first, q, jnp.zeros_like(q)), jnp.where(first, jnp.zeros_like(q), q))

    def score_chunk(u, qc, s_scr, j, mx):
        kblk = k_ref[pl.ds(_aligned((u % nsb) * ks + j * tk, tk), tk), :]
        out = []
        for c in comps:
            s = lax.dot_general(kblk, qc[c], nt_dims, preferred_element_type=F32)
            s_scr[c, j] = s
            out.append(jnp.maximum(mx[c], jnp.max(s.reshape(groups, SUBLANES, tq), axis=0)))
        return out

    def exp_chunk(s_scr, j, m_new, ls):
        out = []
        for c in comps:
            p = jnp.exp2(s_scr[c, j] - m_new[c])
            out.append(ls[c] + jnp.sum(p.reshape(groups, SUBLANES, tq), axis=0))
            p_scr[c, j * tk:(j + 1) * tk, :] = p.astype(BF16)
        return out

    def unit(u, slot, st, with_next):
        mx, m, l = st[0:2], st[2:4], st[4:6]
        sb, qi = u % nsb, u // nsb
        m_new = [jnp.maximum(m[c], jnp.max(mx[c], axis=0, keepdims=True)) for c in comps]
        alpha = [jnp.exp2(m[c] - m_new[c]) for c in comps]
        ls = [jnp.zeros((SUBLANES, tq), F32) for _ in comps]
        mx_next = [jnp.full((SUBLANES, tq), -jnp.inf, F32) for _ in comps]
        qc_next = load_q(u + 1) if with_next else None
        for j in range(nkc):
            if with_next:
                mx_next = score_chunk(u + 1, qc_next, s_bufs[1 - slot], j, mx_next)
            ls = exp_chunk(s_bufs[slot], j, m_new, ls)
        l_new = [alpha[c] * l[c] + jnp.sum(ls[c], axis=0, keepdims=True) for c in comps]
        vt = vt_scr[sb]
        for c in comps:
            acc_scr[c] = alpha[c] * acc_scr[c] + jnp.dot(vt, p_scr[c], preferred_element_type=F32)

        last = jnp.asarray(sb == nsb - 1)

        @pl.when(last)
        def _():
            o_t = acc_scr[0] * (1.0 / l_new[0]) - lam * (acc_scr[1] * (1.0 / l_new[1]))
            ms = jnp.mean(o_t * o_t, axis=0, keepdims=True)
            y_t = ((o_t * lax.rsqrt(ms + EPS)) * gain) * (1.0 - lam_init)
            o_ref[pl.ds(_aligned(qi * tq, tq), tq), :] = y_t.T.astype(BF16)
            acc_scr[...] = jnp.zeros_like(acc_scr)

        m_out = [jnp.where(last, -jnp.inf, m_new[c]) for c in comps]
        l_out = [jnp.where(last, 0.0, l_new[c]) for c in comps]
        return (mx_next[0], mx_next[1], m_out[0], m_out[1], l_out[0], l_out[1])

    acc_scr[...] = jnp.zeros_like(acc_scr)
    mx0 = [jnp.full((SUBLANES, tq), -jnp.inf, F32) for _ in comps]
    qc0 = load_q(0)
    for j in range(nkc):
        mx0 = score_chunk(0, qc0, s_bufs[0], j, mx0)
    neg = jnp.full((1, tq), -jnp.inf, F32)
    zero = jnp.zeros((1, tq), F32)
    st = (mx0[0], mx0[1], neg, neg, zero, zero)

    def pair(i, st):
        st = unit(2 * i, 0, st, True)
        return unit(2 * i + 1, 1, st, True)

    st = lax.fori_loop(0, n_units // 2 - 1, pair, st)
    st = unit(n_units - 2, 0, st, True)
    unit(n_units - 1, 1, st, False)


def _attention(qkv, lq, gain_col, row0, nb, seq, lam_init):
    blk0 = row0 // seq
    nh = N_DIFF_HEADS
    tq, tk = ATTN_Q_TILE, ATTN_K_TILE
    ks = min(seq, ATTN_K_SUPER)
    kern = functools.partial(_attn_kernel, seq=seq, lam_init=lam_init)
    return pl.pallas_call(
        kern,
        grid=(nb, nh),
        in_specs=[
            pl.BlockSpec((4, DIFF_HEAD_DIM), lambda b, h: (0, 0)),
            pl.BlockSpec((seq, LANES), lambda b, h: (blk0 + b, h)),
            pl.BlockSpec((seq, LANES), lambda b, h: (blk0 + b, nh + h)),
            pl.BlockSpec((seq, LANES), lambda b, h: (blk0 + b, 2 * nh + h)),
            pl.BlockSpec((DIFF_V_DIM, 1), lambda b, h: (0, 0)),
        ],
        out_specs=pl.BlockSpec((seq, LANES), lambda b, h: (b, h)),
        out_shape=jax.ShapeDtypeStruct((nb * seq, ATTN_WIDTH), BF16),
        scratch_shapes=[
            pltpu.VMEM((seq // ks, DIFF_V_DIM, ks), BF16),
            pltpu.VMEM((2, ks // tk, tk, tq), F32),
            pltpu.VMEM((2, ks // tk, tk, tq), F32),
            pltpu.VMEM((2, ks, tq), BF16),
            pltpu.VMEM((2, DIFF_V_DIM, tq), F32),
        ],
        compiler_params=_cparams(("arbitrary", "arbitrary")),
        name="diff_attention",
    )(lq, qkv, qkv, qkv, gain_col)


def _dft_kernel(c_ref, s_ref, y_ref, o_ref, acc_scr, *, nk, scale):
    k = pl.program_id(2)
    y = y_ref[...]
    part = (jnp.dot(c_ref[...], y[:, 0:FOURIER_WIDTH], preferred_element_type=F32)
            + jnp.dot(s_ref[...], y[:, FOURIER_WIDTH:2 * FOURIER_WIDTH], preferred_element_type=F32))

    @pl.when(k == 0)
    def _():
        acc_scr[...] = part

    @pl.when(k > 0)
    def _():
        acc_scr[...] += part

    @pl.when(k == nk - 1)
    def _():
        o_ref[...] = (acc_scr[...] * scale).astype(BF16)


def _fourier(yf, cmat, smat, row0, nb, seq):
    tm = min(seq, DFT_ROW_TILE)
    tk = min(seq, DFT_K_TILE)
    ni, nk = seq // tm, seq // tk
    yblk0 = row0 // tk
    kern = functools.partial(_dft_kernel, nk=nk, scale=1.0 / math.sqrt(seq))
    return pl.pallas_call(
        kern,
        grid=(ni, nb, nk),
        in_specs=[
            pl.BlockSpec((tm, tk), lambda i, b, k: (i, k)),
            pl.BlockSpec((tm, tk), lambda i, b, k: (i, k)),
            pl.BlockSpec((tk, 2 * FOURIER_WIDTH), lambda i, b, k: (yblk0 + b * nk + k, 0)),
        ],
        out_specs=pl.BlockSpec((tm, FOURIER_WIDTH), lambda i, b, k: (b * ni + i, 0)),
        out_shape=jax.ShapeDtypeStruct((nb * seq, FOURIER_WIDTH), BF16),
        scratch_shapes=[pltpu.VMEM((tm, FOURIER_WIDTH), F32)],
        compiler_params=_cparams(("arbitrary", "arbitrary", "arbitrary")),
        name="fourier_dft",
    )(cmat, smat, yf)


def _mixed_residual(xa_ref, xb_ref, ofa_ref, ofb_ref, oaa_ref, oab_ref, w_ref, g_ref, x_split, o_split):
    y = (jnp.dot(o_split.pick(ofa_ref, ofb_ref), w_ref[0:FOURIER_WIDTH, :], preferred_element_type=F32)
         + jnp.dot(o_split.pick(oaa_ref, oab_ref), w_ref[FOURIER_WIDTH:D_MODEL, :], preferred_element_type=F32))
    x1 = x_split.pick(xa_ref, xb_ref) + y
    ms = jnp.mean(x1 * x1, axis=-1, keepdims=True)
    return x1, (x1 * lax.rsqrt(ms + EPS)) * g_ref[...]


def _outproj_ffn_kernel(xa_ref, xb_ref, ofa_ref, ofb_ref, oaa_ref, oab_ref, w_ref, g_ref, wgu_ref, wd_ref,
                        o_ref, h_scr, *, x_split, o_split):
    x1, h = _mixed_residual(xa_ref, xb_ref, ofa_ref, ofb_ref, oaa_ref, oab_ref, w_ref, g_ref, x_split, o_split)
    o_ref[...] = x1
    h_scr[...] = h.astype(BF16)
    hb = h_scr[...]
    fc = FFN_CHUNK
    for j in range(FF_DENSE // fc):
        g = jnp.dot(hb, wgu_ref[:, j * fc:(j + 1) * fc], preferred_element_type=F32)
        u = jnp.dot(hb, wgu_ref[:, FF_DENSE + j * fc:FF_DENSE + (j + 1) * fc], preferred_element_type=F32)
        o_ref[...] += jnp.dot(_silu_mul(g, u).astype(BF16), wd_ref[j * fc:(j + 1) * fc, :],
                              preferred_element_type=F32)


def _outproj_router_kernel(xa_ref, xb_ref, ofa_ref, ofb_ref, oaa_ref, oab_ref, w_ref, g_ref, rw_ref,
                           x1_ref, h_ref, rt_ref, cnt_ref, carry_scr, *, x_split, o_split):
    x1, h = _mixed_residual(xa_ref, xb_ref, ofa_ref, ofb_ref, oaa_ref, oab_ref, w_ref, g_ref, x_split, o_split)
    x1_ref[...] = x1
    h_ref[...] = h.astype(BF16)
    _route(h, rw_ref, rt_ref, cnt_ref, carry_scr)


def _resident(shape):
    return pl.BlockSpec(shape, lambda i: (0,) * len(shape), pipeline_mode=pl.Buffered(1))


def _mixed_specs(x_split, o_split, tm):
    return (x_split.specs(tm, D_MODEL) + o_split.specs(tm, FOURIER_WIDTH) + o_split.specs(tm, ATTN_WIDTH)
            + [_resident((D_MODEL, D_MODEL)), pl.BlockSpec((1, D_MODEL), lambda i: (0, 0))])


def _outproj_ffn(xa, xb, x_split, o_f, o_a, o_split, n, w_bf, g, wgu_bf, wd_bf):
    tm = TOKEN_TILE
    return pl.pallas_call(
        functools.partial(_outproj_ffn_kernel, x_split=x_split, o_split=o_split),
        grid=(n // tm,),
        in_specs=_mixed_specs(x_split, o_split, tm) + [_resident((D_MODEL, 2 * FF_DENSE)),
                                                      _resident((FF_DENSE, D_MODEL))],
        out_specs=pl.BlockSpec((tm, D_MODEL), lambda i: (i, 0)),
        out_shape=jax.ShapeDtypeStruct((n, D_MODEL), F32),
        scratch_shapes=[pltpu.VMEM((tm, D_MODEL), BF16)],
        compiler_params=_cparams(("arbitrary",)),
        name="outproj_dense_swiglu",
    )(xa, xb, o_f[0], o_f[1], o_a[0], o_a[1], w_bf, g, wgu_bf, wd_bf)


def _outproj_router(xa, xb, x_split, o_f, o_a, o_split, n, w_bf, g, rw_t):
    tm = TOKEN_TILE
    return pl.pallas_call(
        functools.partial(_outproj_router_kernel, x_split=x_split, o_split=o_split),
        grid=(n // tm,),
        in_specs=_mixed_specs(x_split, o_split, tm) + [pl.BlockSpec((N_EXPERTS, D_MODEL), lambda i: (0, 0))],
        out_specs=[
            pl.BlockSpec((tm, D_MODEL), lambda i: (i, 0)),
            pl.BlockSpec((tm, D_MODEL), lambda i: (i, 0)),
            pl.BlockSpec((SUBLANES, tm), lambda i: (0, i)),
            pl.BlockSpec((N_EXPERTS, LANES), lambda i: (0, 0)),
        ],
        out_shape=[
            jax.ShapeDtypeStruct((n, D_MODEL), F32),
            jax.ShapeDtypeStruct((n, D_MODEL), BF16),
            jax.ShapeDtypeStruct((SUBLANES, n), F32),
            jax.ShapeDtypeStruct((N_EXPERTS, LANES), F32),
        ],
        scratch_shapes=[pltpu.VMEM((N_EXPERTS, LANES), F32)],
        compiler_params=_cparams(("arbitrary",)),
        name="outproj_router",
    )(xa, xb, o_f[0], o_f[1], o_a[0], o_a[1], w_bf, g, rw_t)


def _route(h, rw_ref, rt_ref, cnt_ref, carry_scr):
    tm = TOKEN_TILE

    @pl.when(pl.program_id(0) == 0)
    def _():
        carry_scr[...] = jnp.zeros_like(carry_scr)

    h_hi = h.astype(BF16)
    h_lo = (h - h_hi.astype(F32)).astype(BF16)
    rw = rw_ref[...]
    rw_hi = rw.astype(BF16)
    rw_lo = (rw - rw_hi.astype(F32)).astype(BF16)
    nt_dims = (((1,), (1,)), ((), ()))
    dg = lambda a, b: lax.dot_general(a, b, nt_dims, preferred_element_type=F32)
    logits = dg(rw_hi, h_hi) + (dg(rw_hi, h_lo) + dg(rw_lo, h_hi))

    eidx = lax.broadcasted_iota(jnp.int32, (N_EXPERTS, tm), 0)
    m1 = jnp.max(logits, axis=0, keepdims=True)
    i1 = jnp.min(jnp.where(logits == m1, eidx, N_EXPERTS), axis=0, keepdims=True)
    oh1 = eidx == i1
    rest = jnp.where(oh1, -jnp.inf, logits)
    m2 = jnp.max(rest, axis=0, keepdims=True)
    i2 = jnp.min(jnp.where(rest == m2, eidx, N_EXPERTS), axis=0, keepdims=True)
    oh2 = eidx == i2
    e = jnp.exp(m2 - m1)
    g1 = 1.0 / (1.0 + e)
    g2 = e / (1.0 + e)

    onehot = jnp.where(oh1 | oh2, 1.0, 0.0)
    upper = (lax.broadcasted_iota(jnp.int32, (tm, tm), 0)
             < lax.broadcasted_iota(jnp.int32, (tm, tm), 1))
    before = jnp.dot(onehot.astype(BF16), jnp.where(upper, 1.0, 0.0).astype(BF16),
                     preferred_element_type=F32)
    rank = before + carry_scr[:, 0:1]
    r1 = jnp.sum(jnp.where(oh1, rank, 0.0), axis=0, keepdims=True)
    r2 = jnp.sum(jnp.where(oh2, rank, 0.0), axis=0, keepdims=True)
    zero = jnp.zeros_like(g1)
    rt_ref[...] = jnp.concatenate(
        [i1.astype(F32), i2.astype(F32), r1, r2, g1, g2, zero, zero], axis=0)
    total = carry_scr[...] + jnp.sum(onehot, axis=1, keepdims=True)
    carry_scr[...] = total
    cnt_ref[...] = total


def _row_copy(src, src_row, dst, dst_row, sem):
    return pltpu.make_async_copy(src.at[pl.ds(src_row, 1)], dst.at[pl.ds(dst_row, 1)], sem)


def _dispatch_kernel(s1_ref, s2_ref, h_ref, xs_ref, rows_scr, sem):
    tm = TOKEN_TILE
    rows_scr[...] = h_ref[...].astype(F32)

    def issue(t, c):
        _row_copy(rows_scr, t, xs_ref, s1_ref[t], sem).start()
        _row_copy(rows_scr, t, xs_ref, s2_ref[t], sem).start()
        return c

    lax.fori_loop(0, tm, issue, 0, unroll=ROW_DMA_UNROLL)
    for _ in range(2):
        pltpu.make_async_copy(rows_scr, xs_ref.at[pl.ds(0, tm)], sem).wait()


def _dispatch(slot1, slot2, h, n_slots):
    n = h.shape[0]
    tm = TOKEN_TILE
    smem = lambda: pl.BlockSpec((tm,), lambda i: (i,), memory_space=pltpu.SMEM)
    return pl.pallas_call(
        _dispatch_kernel,
        grid=(n // tm,),
        in_specs=[smem(), smem(), pl.BlockSpec((tm, D_MODEL), lambda i: (i, 0))],
        out_specs=pl.BlockSpec(memory_space=pl.ANY),
        out_shape=jax.ShapeDtypeStruct((n_slots, D_MODEL), F32),
        scratch_shapes=[pltpu.VMEM((tm, D_MODEL), F32), pltpu.SemaphoreType.DMA(())],
        compiler_params=_cparams(("arbitrary",)),
        name="moe_dispatch",
    )(slot1, slot2, h)


def _expert_kernel(te_ref, nt_ref, tv_ref, xs_ref, wg_ref, wu_ref, wd_ref, ys_ref, xb_scr):
    i = pl.program_id(0)
    j = pl.program_id(1)

    @pl.when(i < nt_ref[0])
    def _():
        @pl.when(j == 0)
        def _():
            rows = lax.broadcasted_iota(jnp.int32, (EXPERT_TILE, 1), 0)
            xb_scr[...] = jnp.where(rows < tv_ref[i], xs_ref[...], 0.0).astype(BF16)
            ys_ref[...] = jnp.zeros_like(ys_ref)

        xb = xb_scr[...]
        fc = FFN_CHUNK
        for t in range(EXPERT_CHUNK // fc):
            g = jnp.dot(xb, wg_ref[0, :, t * fc:(t + 1) * fc], preferred_element_type=F32)
            u = jnp.dot(xb, wu_ref[0, :, t * fc:(t + 1) * fc], preferred_element_type=F32)
            ys_ref[...] += jnp.dot(_silu_mul(g, u).astype(BF16), wd_ref[0, t * fc:(t + 1) * fc, :],
                                   preferred_element_type=F32)


def _experts(tile_expert, n_tiles, tile_valid, xs, wgu_bf, wd_bf):
    n_slots = xs.shape[0]
    te, fc = EXPERT_TILE, EXPERT_CHUNK
    nj = FF_EXPERT // fc
    max_tiles = n_slots // te

    def tile(i, nt):
        return jnp.minimum(i, nt[0] - 1)

    def chunk(i, j, nt):
        return jnp.where(i < nt[0], j, nj - 1)

    grid_spec = pltpu.PrefetchScalarGridSpec(
        num_scalar_prefetch=3,
        grid=(max_tiles, nj),
        in_specs=[
            pl.BlockSpec((te, D_MODEL), lambda i, j, e, nt, tv: (tile(i, nt), 0)),
            pl.BlockSpec((1, D_MODEL, fc), lambda i, j, e, nt, tv: (e[tile(i, nt)], 0, chunk(i, j, nt))),
            pl.BlockSpec((1, D_MODEL, fc), lambda i, j, e, nt, tv: (e[tile(i, nt)], 0, nj + chunk(i, j, nt))),
            pl.BlockSpec((1, fc, D_MODEL), lambda i, j, e, nt, tv: (e[tile(i, nt)], chunk(i, j, nt), 0)),
        ],
        out_specs=pl.BlockSpec((te, D_MODEL), lambda i, j, e, nt, tv: (tile(i, nt), 0)),
        scratch_shapes=[pltpu.VMEM((te, D_MODEL), BF16)],
    )
    return pl.pallas_call(
        _expert_kernel,
        grid_spec=grid_spec,
        out_shape=jax.ShapeDtypeStruct((n_slots, D_MODEL), F32),
        compiler_params=_cparams(("arbitrary", "arbitrary")),
        name="expert_swiglu",
    )(tile_expert, n_tiles, tile_valid, xs, wgu_bf, wgu_bf, wd_bf)


def _combine_kernel(s1_ref, s2_ref, rt_ref, x1_ref, g_ref, ys_ref, op_ref, os_ref, a_scr, b_scr, sem, *,
                    tiles_p):
    tm = TOKEN_TILE
    i = pl.program_id(0)

    def issue(t, c):
        _row_copy(ys_ref, s1_ref[t], a_scr, t, sem).start()
        _row_copy(ys_ref, s2_ref[t], b_scr, t, sem).start()
        return c

    lax.fori_loop(0, tm, issue, 0, unroll=ROW_DMA_UNROLL)

    rt = jnp.concatenate([rt_ref[...], jnp.zeros((LANES - SUBLANES, tm), F32)], axis=0)
    rt_t = rt.T
    g1 = rt_t[:, 4:5]
    g2 = rt_t[:, 5:6]

    pltpu.make_async_copy(ys_ref.at[pl.ds(0, tm)], a_scr, sem).wait()
    pltpu.make_async_copy(ys_ref.at[pl.ds(0, tm)], b_scr, sem).wait()

    x2 = x1_ref[...] + (g1 * a_scr[...] + g2 * b_scr[...])
    ms = jnp.mean(x2 * x2, axis=-1, keepdims=True)
    y = (x2 * lax.rsqrt(ms + EPS)) * g_ref[...]

    @pl.when(i < tiles_p)
    def _():
        op_ref[...] = y

    @pl.when(i >= tiles_p)
    def _():
        os_ref[...] = y


def _combine(slot1, slot2, rt, x1, g_final, ys, n_p):
    n = x1.shape[0]
    tm = TOKEN_TILE
    tiles_p = n_p // tm
    smem = lambda: pl.BlockSpec((tm,), lambda i: (i,), memory_space=pltpu.SMEM)
    return pl.pallas_call(
        functools.partial(_combine_kernel, tiles_p=tiles_p),
        grid=(n // tm,),
        in_specs=[
            smem(), smem(),
            pl.BlockSpec((SUBLANES, tm), lambda i: (0, i)),
            pl.BlockSpec((tm, D_MODEL), lambda i: (i, 0)),
            pl.BlockSpec((1, D_MODEL), lambda i: (0, 0)),
            pl.BlockSpec(memory_space=pl.ANY),
        ],
        out_specs=[
            pl.BlockSpec((tm, D_MODEL), lambda i: (jnp.minimum(i, tiles_p - 1), 0)),
            pl.BlockSpec((tm, D_MODEL), lambda i: (jnp.maximum(i - tiles_p, 0), 0)),
        ],
        out_shape=[
            jax.ShapeDtypeStruct((n_p, D_MODEL), F32),
            jax.ShapeDtypeStruct((n - n_p, D_MODEL), F32),
        ],
        scratch_shapes=[
            pltpu.VMEM((tm, D_MODEL), F32),
            pltpu.VMEM((tm, D_MODEL), F32),
            pltpu.SemaphoreType.DMA(()),
        ],
        compiler_params=_cparams(("arbitrary",)),
        name="moe_combine_norm",
    )(slot1, slot2, rt, x1, g_final, ys)


def _rope_tables(seq_max):
    half = ROT_DIM // 2
    inv = ROPE_THETA ** (-jnp.arange(0, ROT_DIM, 2, dtype=F32) / ROT_DIM)
    ang = jnp.arange(seq_max, dtype=F32)[:, None] * inv[None, :]
    cos, sin = jnp.cos(ang), jnp.sin(ang)
    ones = jnp.ones((seq_max, DIFF_HEAD_DIM - ROT_DIM), F32)
    zeros = jnp.zeros((seq_max, DIFF_HEAD_DIM - ROT_DIM), F32)
    z8 = jnp.zeros((seq_max, half), F32)
    ct = jnp.concatenate([cos, cos, ones], axis=1)
    sa = jnp.concatenate([-sin, z8, zeros], axis=1)
    sb = jnp.concatenate([z8, sin, zeros], axis=1)
    reps = 2 * LANES // DIFF_HEAD_DIM
    return tuple(jnp.tile(t, (1, reps)) for t in (ct, sa, sb))


def _channel_dft_matrix():
    c = np.arange(FOURIER_GROUP_DIM)
    ang = 2.0 * np.pi * ((c[:, None] * c[None, :]) % FOURIER_GROUP_DIM) / FOURIER_GROUP_DIM
    scale = FOURIER_GROUP_DIM ** -0.5
    eye = np.eye(FOURIER_GROUPS)
    bd = np.concatenate([np.kron(eye, np.cos(ang) * scale), np.kron(eye, np.sin(ang) * scale)], axis=1)
    return jnp.asarray(bd, dtype=BF16)


def _position_dft_matrices(seq):
    lo = FOURIER_GROUP_DIM
    hi = seq // lo
    sp = jnp.arange(seq, dtype=jnp.int32)[None, :]
    a = jnp.arange(hi, dtype=jnp.int32)[:, None]
    b = jnp.arange(lo, dtype=jnp.int32)[:, None]
    w = 2.0 * math.pi / seq
    ang_a = ((a * lo * sp) % seq).astype(F32) * w
    ang_b = ((b * sp) % seq).astype(F32) * w
    ca, sa = jnp.cos(ang_a)[:, None, :], jnp.sin(ang_a)[:, None, :]
    cb, sb = jnp.cos(ang_b)[None, :, :], jnp.sin(ang_b)[None, :, :]
    cmat = (ca * cb - sa * sb).reshape(seq, seq).astype(BF16)
    smat = (-(sa * cb + ca * sb)).reshape(seq, seq).astype(BF16)
    return cmat, smat


def kernel(x_prompt, x_sample, norm_mix, w_in, lambda_qk, subln_gain, w_out, norm_ffn, ffn_w_gate_up,
           ffn_w_down, router_w, expert_w_gate_up, expert_w_down, final_norm):
    nb_p, seq_p, _ = x_prompt.shape
    nb_s, seq_s, _ = x_sample.shape
    n_p, n_s = nb_p * seq_p, nb_s * seq_s
    n = n_p + n_s
    tm = TOKEN_TILE
    assert seq_p % tm == 0 and seq_s % tm == 0
    assert n_p % seq_s == 0 and seq_p % ATTN_K_TILE == 0 and seq_s % ATTN_K_TILE == 0
    assert n_p % min(seq_s, DFT_K_TILE) == 0
    assert DEPTH == 2

    ct, sa, sb = _rope_tables(max(seq_p, seq_s))
    bd = _channel_dft_matrix()
    dft_p = _position_dft_matrices(seq_p)
    dft_s = dft_p if seq_s == seq_p else _position_dft_matrices(seq_s)

    tiles_p = n_p // tm
    two_arrays = _Split(tiles_p, 0)
    one_array = _Split(tiles_p, tiles_p)
    xa, xb, x_split = x_prompt.reshape(n_p, D_MODEL), x_sample.reshape(n_s, D_MODEL), two_arrays
    out = None
    for layer in range(DEPTH):
        lam_init = 0.8 - 0.6 * math.exp(-0.3 * layer)
        yf, qkv = _norm_inproj(xa, xb, x_split, n, norm_mix[layer][None, :], w_in[layer].astype(BF16),
                               bd, ct, sa, sb, seq_p, seq_s)
        gain_col = subln_gain[layer][:, None]
        o_a = (_attention(qkv, lambda_qk[layer], gain_col, 0, nb_p, seq_p, lam_init),
               _attention(qkv, lambda_qk[layer], gain_col, n_p, nb_s, seq_s, lam_init))
        o_f = (_fourier(yf, dft_p[0], dft_p[1], 0, nb_p, seq_p),
               _fourier(yf, dft_s[0], dft_s[1], n_p, nb_s, seq_s))
        mixed = (xa, xb, x_split, o_f, o_a, two_arrays, n, w_out[layer].astype(BF16), norm_ffn[layer][None, :])
        if layer % 2 == 0:
            x = _outproj_ffn(*mixed, ffn_w_gate_up[layer // 2].astype(BF16), ffn_w_down[layer // 2].astype(BF16))
            xa, xb, x_split = x, x, one_array
        else:
            j = layer // 2
            x1, h, rt, cnt = _outproj_router(*mixed, router_w[j].T)
            te = EXPERT_TILE
            counts = cnt[:, 0].astype(jnp.int32)
            padded = ((counts + te - 1) // te) * te
            ends = jnp.cumsum(padded)
            offs = ends - padded
            e1, e2 = rt[0].astype(jnp.int32), rt[1].astype(jnp.int32)
            slot1 = offs[e1] + rt[2].astype(jnp.int32)
            slot2 = offs[e2] + rt[3].astype(jnp.int32)
            n_slots = 2 * n + N_EXPERTS * te
            starts = jnp.arange(n_slots // te, dtype=jnp.int32) * te
            tile_expert = jnp.minimum(jnp.sum((starts[:, None] >= ends[None, :]).astype(jnp.int32), axis=1),
                                      N_EXPERTS - 1)
            tile_valid = jnp.clip(counts[tile_expert] - (starts - offs[tile_expert]), 0, te).astype(jnp.int32)
            n_tiles = (ends[-1:] // te).astype(jnp.int32)
            xs = _dispatch(slot1, slot2, h, n_slots)
            ys = _experts(tile_expert, n_tiles, tile_valid, xs,
                          expert_w_gate_up[j].astype(BF16), expert_w_down[j].astype(BF16))
            out = _combine(slot1, slot2, rt, x1, final_norm[None, :], ys, n_p)
    y_p, y_s = out
    return (y_p.reshape(nb_p, seq_p, D_MODEL), y_s.reshape(nb_s, seq_s, D_MODEL))
```

```python
import functools
import math

import numpy as np
import jax
import jax.numpy as jnp
from jax import lax
from jax.experimental import pallas as pl
from jax.experimental.pallas import tpu as pltpu

F32 = jnp.float32
BF16 = jnp.bfloat16

D_MODEL = 1024
DEPTH = 2
FOURIER_WIDTH = 256
FOURIER_GROUPS = 4
FOURIER_GROUP_DIM = 64
ATTN_WIDTH = 768
DIFF_HEAD_DIM = 64
DIFF_V_DIM = 128
N_DIFF_HEADS = 6
QK_WIDTH = 768
IN_PROJ_WIDTH = 2560
ROT_DIM = 16
ROPE_THETA = 500000.0
FF_DENSE = 2816
N_EXPERTS = 8
FF_EXPERT = 3584
EPS = 1e-5

LANES = 128
SUBLANES = 8
VMEM_LIMIT_BYTES = 56 * 1024 * 1024

TOKEN_TILE = 512
ATTN_Q_TILE = 512
ATTN_K_TILE = 512
ATTN_K_SUPER = 2048
LOG2E = 1.4426950408889634
DFT_ROW_TILE = 1024
DFT_K_TILE = 2048
DFT_MAX_SEQS_PER_STEP = 4
FFN_CHUNK = 256
EXPERT_TILE = 1024
EXPERT_CHUNK = 1792
ROW_DMA_UNROLL = 8


def _cparams(sem):
    return pltpu.CompilerParams(dimension_semantics=sem, vmem_limit_bytes=VMEM_LIMIT_BYTES)


def _aligned(start, multiple):
    return start if isinstance(start, int) else pl.multiple_of(start, multiple)


def _silu_mul(g, u):
    return (g * (1.0 / (1.0 + jnp.exp(-g)))) * u


class _Split:
    def __init__(self, tiles_a, b_blk0):
        self.tiles_a, self.b_blk0 = tiles_a, b_blk0

    def specs(self, tm, width):
        ta, b0 = self.tiles_a, self.b_blk0
        return [pl.BlockSpec((tm, width), lambda i: (jnp.minimum(i, ta - 1), 0)),
                pl.BlockSpec((tm, width), lambda i: (b0 + jnp.maximum(i - ta, 0), 0))]

    def pick(self, a_ref, b_ref):
        return jnp.where(pl.program_id(0) < self.tiles_a, a_ref[...], b_ref[...])


def _norm_inproj_kernel(xa_ref, xb_ref, g_ref, w_ref, bd_ref, ct_ref, sa_ref, sb_ref, yf_ref, qkv_ref, h_scr,
                        *, split):
    x = split.pick(xa_ref, xb_ref)
    ms = jnp.mean(x * x, axis=-1, keepdims=True)
    h_scr[...] = ((x * lax.rsqrt(ms + EPS)) * g_ref[...]).astype(BF16)
    hb = h_scr[...]
    uf = jnp.dot(hb, w_ref[:, 0:FOURIER_WIDTH], preferred_element_type=F32)
    yf_ref[...] = jnp.dot(uf.astype(BF16), bd_ref[...], preferred_element_type=F32).astype(BF16)
    ct, sa, sb = ct_ref[...], sa_ref[...], sb_ref[...]
    width = 2 * LANES
    for blk in range(2 * QK_WIDTH // width):
        c0 = FOURIER_WIDTH + blk * width
        t = jnp.dot(hb, w_ref[:, c0:c0 + width], preferred_element_type=F32)
        r = t * ct + pltpu.roll(t, width - ROT_DIM // 2, 1) * sa + pltpu.roll(t, ROT_DIM // 2, 1) * sb
        if blk < QK_WIDTH // width:
            r = r * (DIFF_HEAD_DIM ** -0.5 * LOG2E)
        qkv_ref[:, blk * width:(blk + 1) * width] = r.astype(BF16)
    v0 = FOURIER_WIDTH + 2 * QK_WIDTH
    for blk in range(ATTN_WIDTH // width):
        t = jnp.dot(hb, w_ref[:, v0 + blk * width:v0 + (blk + 1) * width], preferred_element_type=F32)
        qkv_ref[:, 2 * QK_WIDTH + blk * width:2 * QK_WIDTH + (blk + 1) * width] = t.astype(BF16)


def _norm_inproj(xa, xb, split, n, g, w_bf, bd, ct, sa, sb, seq_p, seq_s):
    tm = TOKEN_TILE
    tiles_p = split.tiles_a
    per_p, per_s = seq_p // tm, seq_s // tm

    def pos_map(i):
        return (jnp.where(i < tiles_p, i % per_p, (i - tiles_p) % per_s), 0)

    tab = pl.BlockSpec((tm, 2 * LANES), pos_map)
    return pl.pallas_call(
        functools.partial(_norm_inproj_kernel, split=split),
        grid=(n // tm,),
        in_specs=split.specs(tm, D_MODEL) + [
            pl.BlockSpec((1, D_MODEL), lambda i: (0, 0)),
            pl.BlockSpec((D_MODEL, IN_PROJ_WIDTH), lambda i: (0, 0)),
            pl.BlockSpec((FOURIER_WIDTH, 2 * FOURIER_WIDTH), lambda i: (0, 0)),
            tab, tab, tab,
        ],
        out_specs=[
            pl.BlockSpec((tm, 2 * FOURIER_WIDTH), lambda i: (i, 0)),
            pl.BlockSpec((tm, 2 * QK_WIDTH + ATTN_WIDTH), lambda i: (i, 0)),
        ],
        out_shape=[
            jax.ShapeDtypeStruct((n, 2 * FOURIER_WIDTH), BF16),
            jax.ShapeDtypeStruct((n, 2 * QK_WIDTH + ATTN_WIDTH), BF16),
        ],
        scratch_shapes=[pltpu.VMEM((tm, D_MODEL), BF16)],
        compiler_params=_cparams(("arbitrary",)),
        name="norm_inproj",
    )(xa, xb, g, w_bf, bd, ct, sa, sb)


def _attn_kernel(lq_ref, q_ref, k_ref, v_ref, g_ref, o_ref, vt_scr, sa_scr, sb_scr, p_scr, acc_scr, *,
                 seq, lam_init):
    tq, tk = ATTN_Q_TILE, ATTN_K_TILE
    ks = min(seq, ATTN_K_SUPER)
    nsb, nkc = seq // ks, ks // tk
    groups = tk // SUBLANES
    n_units = (seq // tq) * nsb
    assert n_units % 2 == 0
    s_bufs = (sa_scr, sb_scr)
    for sb in range(nsb):
        for j in range(nkc):
            r0 = sb * ks + j * tk
            vt_scr[sb, :, j * tk:(j + 1) * tk] = v_ref[r0:r0 + tk, :].astype(F32).T.astype(BF16)

    lq = lq_ref[...]
    lam = (jnp.exp(jnp.sum(lq[0:1, :] * lq[1:2, :], axis=1, keepdims=True))
           - jnp.exp(jnp.sum(lq[2:3, :] * lq[3:4, :], axis=1, keepdims=True)) + lam_init)
    gain = g_ref[...]
    first = lax.broadcasted_iota(jnp.int32, (1, DIFF_V_DIM), 1) < DIFF_HEAD_DIM
    nt_dims = (((1,), (1,)), ((), ()))
    comps = (0, 1)

    def load_q(u):
        q = q_ref[pl.ds(_aligned((u // nsb) * tq, tq), tq), :]
        return (jnp.where(first, q, jnp.zeros_like(q)), jnp.where(first, jnp.zeros_like(q), q))

    def score_chunk(u, qc, s_scr, j, mx):
        kblk = k_ref[pl.ds(_aligned((u % nsb) * ks + j * tk, tk), tk), :]
        out = []
        for c in comps:
            s = lax.dot_general(kblk, qc[c], nt_dims, preferred_element_type=F32)
            s_scr[c, j] = s
            out.append(jnp.maximum(mx[c], jnp.max(s.reshape(groups, SUBLANES, tq), axis=0)))
        return out

    def exp_chunk(s_scr, j, m_new, ls):
        out = []
        for c in comps:
            p = jnp.exp2(s_scr[c, j] - m_new[c])
            out.append(ls[c] + jnp.sum(p.reshape(groups, SUBLANES, tq), axis=0))
            p_scr[c, j * tk:(j + 1) * tk, :] = p.astype(BF16)
        return out

    def unit(u, slot, st, with_next):
        mx, m, l = st[0:2], st[2:4], st[4:6]
        sb, qi = u % nsb, u // nsb
        m_new = [jnp.maximum(m[c], jnp.max(mx[c], axis=0, keepdims=True)) for c in comps]
        alpha = [jnp.exp2(m[c] - m_new[c]) for c in comps]
        ls = [jnp.zeros((SUBLANES, tq), F32) for _ in comps]
        mx_next = [jnp.full((SUBLANES, tq), -jnp.inf, F32) for _ in comps]
        qc_next = load_q(u + 1) if with_next else None
        for j in range(nkc):
            if with_next:
                mx_next = score_chunk(u + 1, qc_next, s_bufs[1 - slot], j, mx_next)
            ls = exp_chunk(s_bufs[slot], j, m_new, ls)
        l_new = [alpha[c] * l[c] + jnp.sum(ls[c], axis=0, keepdims=True) for c in comps]
        vt = vt_scr[sb]
        for c in comps:
            acc_scr[c] = alpha[c] * acc_scr[c] + jnp.dot(vt, p_scr[c], preferred_element_type=F32)

        last = jnp.asarray(sb == nsb - 1)

        @pl.when(last)
        def _():
            o_t = acc_scr[0] * (1.0 / l_new[0]) - lam * (acc_scr[1] * (1.0 / l_new[1]))
            ms = jnp.mean(o_t * o_t, axis=0, keepdims=True)
            y_t = ((o_t * lax.rsqrt(ms + EPS)) * gain) * (1.0 - lam_init)
            o_ref[pl.ds(_aligned(qi * tq, tq), tq), :] = y_t.T.astype(BF16)
            acc_scr[...] = jnp.zeros_like(acc_scr)

        m_out = [jnp.where(last, -jnp.inf, m_new[c]) for c in comps]
        l_out = [jnp.where(last, 0.0, l_new[c]) for c in comps]
        return (mx_next[0], mx_next[1], m_out[0], m_out[1], l_out[0], l_out[1])

    acc_scr[...] = jnp.zeros_like(acc_scr)
    mx0 = [jnp.full((SUBLANES, tq), -jnp.inf, F32) for _ in comps]
    qc0 = load_q(0)
    for j in range(nkc):
        mx0 = score_chunk(0, qc0, s_bufs[0], j, mx0)
    neg = jnp.full((1, tq), -jnp.inf, F32)
    zero = jnp.zeros((1, tq), F32)
    st = (mx0[0], mx0[1], neg, neg, zero, zero)

    def pair(i, st):
        st = unit(2 * i, 0, st, True)
        return unit(2 * i + 1, 1, st, True)

    st = lax.fori_loop(0, n_units // 2 - 1, pair, st)
    st = unit(n_units - 2, 0, st, True)
    unit(n_units - 1, 1, st, False)


def _attention(qkv, lq, gain_col, row0, nb, seq, lam_init):
    blk0 = row0 // seq
    nh = N_DIFF_HEADS
    tq, tk = ATTN_Q_TILE, ATTN_K_TILE
    ks = min(seq, ATTN_K_SUPER)
    kern = functools.partial(_attn_kernel, seq=seq, lam_init=lam_init)
    return pl.pallas_call(
        kern,
        grid=(nb, nh),
        in_specs=[
            pl.BlockSpec((4, DIFF_HEAD_DIM), lambda b, h: (0, 0)),
            pl.BlockSpec((seq, LANES), lambda b, h: (blk0 + b, h)),
            pl.BlockSpec((seq, LANES), lambda b, h: (blk0 + b, nh + h)),
            pl.BlockSpec((seq, LANES), lambda b, h: (blk0 + b, 2 * nh + h)),
            pl.BlockSpec((DIFF_V_DIM, 1), lambda b, h: (0, 0)),
        ],
        out_specs=pl.BlockSpec((seq, LANES), lambda b, h: (b, h)),
        out_shape=jax.ShapeDtypeStruct((nb * seq, ATTN_WIDTH), BF16),
        scratch_shapes=[
            pltpu.VMEM((seq // ks, DIFF_V_DIM, ks), BF16),
            pltpu.VMEM((2, ks // tk, tk, tq), F32),
            pltpu.VMEM((2, ks // tk, tk, tq), F32),
            pltpu.VMEM((2, ks, tq), BF16),
            pltpu.VMEM((2, DIFF_V_DIM, tq), F32),
        ],
        compiler_params=_cparams(("arbitrary", "arbitrary")),
        name="diff_attention",
    )(lq, qkv, qkv, qkv, gain_col)


def _dft_kernel(c_ref, s_ref, *rest, nk, nbb, scale):
    y_refs, o_ref, acc_scr = rest[:nbb], rest[nbb], rest[nbb + 1]
    k = pl.program_id(2)

    @pl.when(k == 0)
    def _():
        acc_scr[...] = jnp.zeros_like(acc_scr)

    c, s = c_ref[...], s_ref[...]
    for bb in range(nbb):
        y = y_refs[bb][...]
        acc_scr[bb] += (jnp.dot(c, y[:, 0:FOURIER_WIDTH], preferred_element_type=F32)
                        + jnp.dot(s, y[:, FOURIER_WIDTH:2 * FOURIER_WIDTH], preferred_element_type=F32))

    @pl.when(k == nk - 1)
    def _():
        o_ref[...] = (acc_scr[...] * scale).astype(BF16)


def _fourier(yf, cmat, smat, row0, nb, seq):
    tm = min(seq, DFT_ROW_TILE)
    tk = min(seq, DFT_K_TILE)
    ni, nk = seq // tm, seq // tk
    nbb = nb if (nk > 1 and nb <= DFT_MAX_SEQS_PER_STEP) else 1
    yblk0 = row0 // tk

    def y_spec(bb):
        return pl.BlockSpec((tk, 2 * FOURIER_WIDTH), lambda i, g, k: (yblk0 + (g * nbb + bb) * nk + k, 0))

    kern = functools.partial(_dft_kernel, nk=nk, nbb=nbb, scale=1.0 / math.sqrt(seq))
    out = pl.pallas_call(
        kern,
        grid=(ni, nb // nbb, nk),
        in_specs=[
            pl.BlockSpec((tm, tk), lambda i, g, k: (i, k)),
            pl.BlockSpec((tm, tk), lambda i, g, k: (i, k)),
        ] + [y_spec(bb) for bb in range(nbb)],
        out_specs=pl.BlockSpec((nbb, tm, FOURIER_WIDTH), lambda i, g, k: (g, i, 0)),
        out_shape=jax.ShapeDtypeStruct((nb, seq, FOURIER_WIDTH), BF16),
        scratch_shapes=[pltpu.VMEM((nbb, tm, FOURIER_WIDTH), F32)],
        compiler_params=_cparams(("arbitrary", "arbitrary", "arbitrary")),
        name="fourier_dft",
    )(cmat, smat, *([yf] * nbb))
    return out.reshape(nb * seq, FOURIER_WIDTH)


def _mixed_residual(xa_ref, xb_ref, ofa_ref, ofb_ref, oaa_ref, oab_ref, w_ref, g_ref, x_split, o_split):
    y = (jnp.dot(o_split.pick(ofa_ref, ofb_ref), w_ref[0:FOURIER_WIDTH, :], preferred_element_type=F32)
         + jnp.dot(o_split.pick(oaa_ref, oab_ref), w_ref[FOURIER_WIDTH:D_MODEL, :], preferred_element_type=F32))
    x1 = x_split.pick(xa_ref, xb_ref) + y
    ms = jnp.mean(x1 * x1, axis=-1, keepdims=True)
    return x1, (x1 * lax.rsqrt(ms + EPS)) * g_ref[...]


def _outproj_ffn_kernel(xa_ref, xb_ref, ofa_ref, ofb_ref, oaa_ref, oab_ref, w_ref, g_ref, wgu_ref, wd_ref,
                        o_ref, h_scr, *, x_split, o_split):
    x1, h = _mixed_residual(xa_ref, xb_ref, ofa_ref, ofb_ref, oaa_ref, oab_ref, w_ref, g_ref, x_split, o_split)
    o_ref[...] = x1
    h_scr[...] = h.astype(BF16)
    hb = h_scr[...]
    fc = FFN_CHUNK
    for j in range(FF_DENSE // fc):
        g = jnp.dot(hb, wgu_ref[:, j * fc:(j + 1) * fc], preferred_element_type=F32)
        u = jnp.dot(hb, wgu_ref[:, FF_DENSE + j * fc:FF_DENSE + (j + 1) * fc], preferred_element_type=F32)
        o_ref[...] += jnp.dot(_silu_mul(g, u).astype(BF16), wd_ref[j * fc:(j + 1) * fc, :],
                              preferred_element_type=F32)


def _outproj_router_kernel(xa_ref, xb_ref, ofa_ref, ofb_ref, oaa_ref, oab_ref, w_ref, g_ref, rw_ref,
                           x1_ref, h_ref, rt_ref, cnt_ref, carry_scr, *, x_split, o_split):
    x1, h = _mixed_residual(xa_ref, xb_ref, ofa_ref, ofb_ref, oaa_ref, oab_ref, w_ref, g_ref, x_split, o_split)
    x1_ref[...] = x1
    h_ref[...] = h.astype(BF16)
    _route(h, rw_ref, rt_ref, cnt_ref, carry_scr)


def _resident(shape):
    return pl.BlockSpec(shape, lambda i: (0,) * len(shape), pipeline_mode=pl.Buffered(1))


def _mixed_specs(x_split, o_split, tm):
    return (x_split.specs(tm, D_MODEL) + o_split.specs(tm, FOURIER_WIDTH) + o_split.specs(tm, ATTN_WIDTH)
            + [_resident((D_MODEL, D_MODEL)), pl.BlockSpec((1, D_MODEL), lambda i: (0, 0))])


def _outproj_ffn(xa, xb, x_split, o_f, o_a, o_split, n, w_bf, g, wgu_bf, wd_bf):
    tm = TOKEN_TILE
    return pl.pallas_call(
        functools.partial(_outproj_ffn_kernel, x_split=x_split, o_split=o_split),
        grid=(n // tm,),
        in_specs=_mixed_specs(x_split, o_split, tm) + [_resident((D_MODEL, 2 * FF_DENSE)),
                                                      _resident((FF_DENSE, D_MODEL))],
        out_specs=pl.BlockSpec((tm, D_MODEL), lambda i: (i, 0)),
        out_shape=jax.ShapeDtypeStruct((n, D_MODEL), F32),
        scratch_shapes=[pltpu.VMEM((tm, D_MODEL), BF16)],
        compiler_params=_cparams(("arbitrary",)),
        name="outproj_dense_swiglu",
    )(xa, xb, o_f[0], o_f[1], o_a[0], o_a[1], w_bf, g, wgu_bf, wd_bf)


def _outproj_router(xa, xb, x_split, o_f, o_a, o_split, n, w_bf, g, rw_t):
    tm = TOKEN_TILE
    return pl.pallas_call(
        functools.partial(_outproj_router_kernel, x_split=x_split, o_split=o_split),
        grid=(n // tm,),
        in_specs=_mixed_specs(x_split, o_split, tm) + [pl.BlockSpec((N_EXPERTS, D_MODEL), lambda i: (0, 0))],
        out_specs=[
            pl.BlockSpec((tm, D_MODEL), lambda i: (i, 0)),
            pl.BlockSpec((tm, D_MODEL), lambda i: (i, 0)),
            pl.BlockSpec((SUBLANES, tm), lambda i: (0, i)),
            pl.BlockSpec((N_EXPERTS, LANES), lambda i: (0, 0)),
        ],
        out_shape=[
            jax.ShapeDtypeStruct((n, D_MODEL), F32),
            jax.ShapeDtypeStruct((n, D_MODEL), BF16),
            jax.ShapeDtypeStruct((SUBLANES, n), F32),
            jax.ShapeDtypeStruct((N_EXPERTS, LANES), F32),
        ],
        scratch_shapes=[pltpu.VMEM((N_EXPERTS, LANES), F32)],
        compiler_params=_cparams(("arbitrary",)),
        name="outproj_router",
    )(xa, xb, o_f[0], o_f[1], o_a[0], o_a[1], w_bf, g, rw_t)


def _route(h, rw_ref, rt_ref, cnt_ref, carry_scr):
    tm = TOKEN_TILE

    @pl.when(pl.program_id(0) == 0)
    def _():
        carry_scr[...] = jnp.zeros_like(carry_scr)

    h_hi = h.astype(BF16)
    h_lo = (h - h_hi.astype(F32)).astype(BF16)
    rw = rw_ref[...]
    rw_hi = rw.astype(BF16)
    rw_lo = (rw - rw_hi.astype(F32)).astype(BF16)
    nt_dims = (((1,), (1,)), ((), ()))
    dg = lambda a, b: lax.dot_general(a, b, nt_dims, preferred_element_type=F32)
    logits = dg(rw_hi, h_hi) + (dg(rw_hi, h_lo) + dg(rw_lo, h_hi))

    eidx = lax.broadcasted_iota(jnp.int32, (N_EXPERTS, tm), 0)
    m1 = jnp.max(logits, axis=0, keepdims=True)
    i1 = jnp.min(jnp.where(logits == m1, eidx, N_EXPERTS), axis=0, keepdims=True)
    oh1 = eidx == i1
    rest = jnp.where(oh1, -jnp.inf, logits)
    m2 = jnp.max(rest, axis=0, keepdims=True)
    i2 = jnp.min(jnp.where(rest == m2, eidx, N_EXPERTS), axis=0, keepdims=True)
    oh2 = eidx == i2
    e = jnp.exp(m2 - m1)
    g1 = 1.0 / (1.0 + e)
    g2 = e / (1.0 + e)

    onehot = jnp.where(oh1 | oh2, 1.0, 0.0)
    upper = (lax.broadcasted_iota(jnp.int32, (tm, tm), 0)
             < lax.broadcasted_iota(jnp.int32, (tm, tm), 1))
    before = jnp.dot(onehot.astype(BF16), jnp.where(upper, 1.0, 0.0).astype(BF16),
                     preferred_element_type=F32)
    rank = before + carry_scr[:, 0:1]
    r1 = jnp.sum(jnp.where(oh1, rank, 0.0), axis=0, keepdims=True)
    r2 = jnp.sum(jnp.where(oh2, rank, 0.0), axis=0, keepdims=True)
    zero = jnp.zeros_like(g1)
    rt_ref[...] = jnp.concatenate(
        [i1.astype(F32), i2.astype(F32), r1, r2, g1, g2, zero, zero], axis=0)
    total = carry_scr[...] + jnp.sum(onehot, axis=1, keepdims=True)
    carry_scr[...] = total
    cnt_ref[...] = total


def _row_copy(src, src_row, dst, dst_row, sem):
    return pltpu.make_async_copy(src.at[pl.ds(src_row, 1)], dst.at[pl.ds(dst_row, 1)], sem)


def _dispatch_kernel(s1_ref, s2_ref, h_ref, xs_ref, rows_scr, sem):
    tm = TOKEN_TILE
    rows_scr[...] = h_ref[...].astype(F32)

    def issue(t, c):
        _row_copy(rows_scr, t, xs_ref, s1_ref[t], sem).start()
        _row_copy(rows_scr, t, xs_ref, s2_ref[t], sem).start()
        return c

    lax.fori_loop(0, tm, issue, 0, unroll=ROW_DMA_UNROLL)
    for _ in range(2):
        pltpu.make_async_copy(rows_scr, xs_ref.at[pl.ds(0, tm)], sem).wait()


def _dispatch(slot1, slot2, h, n_slots):
    n = h.shape[0]
    tm = TOKEN_TILE
    smem = lambda: pl.BlockSpec((tm,), lambda i: (i,), memory_space=pltpu.SMEM)
    return pl.pallas_call(
        _dispatch_kernel,
        grid=(n // tm,),
        in_specs=[smem(), smem(), pl.BlockSpec((tm, D_MODEL), lambda i: (i, 0))],
        out_specs=pl.BlockSpec(memory_space=pl.ANY),
        out_shape=jax.ShapeDtypeStruct((n_slots, D_MODEL), F32),
        scratch_shapes=[pltpu.VMEM((tm, D_MODEL), F32), pltpu.SemaphoreType.DMA(())],
        compiler_params=_cparams(("arbitrary",)),
        name="moe_dispatch",
    )(slot1, slot2, h)


def _expert_kernel(te_ref, nt_ref, tv_ref, xs_ref, wg_ref, wu_ref, wd_ref, ys_ref, xb_scr):
    i = pl.program_id(0)
    j = pl.program_id(1)

    @pl.when(i < nt_ref[0])
    def _():
        @pl.when(j == 0)
        def _():
            rows = lax.broadcasted_iota(jnp.int32, (EXPERT_TILE, 1), 0)
            xb_scr[...] = jnp.where(rows < tv_ref[i], xs_ref[...], 0.0).astype(BF16)
            ys_ref[...] = jnp.zeros_like(ys_ref)

        xb = xb_scr[...]
        fc = FFN_CHUNK
        for t in range(EXPERT_CHUNK // fc):
            g = jnp.dot(xb, wg_ref[0, :, t * fc:(t + 1) * fc], preferred_element_type=F32)
            u = jnp.dot(xb, wu_ref[0, :, t * fc:(t + 1) * fc], preferred_element_type=F32)
            ys_ref[...] += jnp.dot(_silu_mul(g, u).astype(BF16), wd_ref[0, t * fc:(t + 1) * fc, :],
                                   preferred_element_type=F32)


def _experts(tile_expert, n_tiles, tile_valid, xs, wgu_bf, wd_bf):
    n_slots = xs.shape[0]
    te, fc = EXPERT_TILE, EXPERT_CHUNK
    nj = FF_EXPERT // fc
    max_tiles = n_slots // te

    def tile(i, nt):
        return jnp.minimum(i, nt[0] - 1)

    def chunk(i, j, nt):
        return jnp.where(i < nt[0], j, nj - 1)

    grid_spec = pltpu.PrefetchScalarGridSpec(
        num_scalar_prefetch=3,
        grid=(max_tiles, nj),
        in_specs=[
            pl.BlockSpec((te, D_MODEL), lambda i, j, e, nt, tv: (tile(i, nt), 0)),
            pl.BlockSpec((1, D_MODEL, fc), lambda i, j, e, nt, tv: (e[tile(i, nt)], 0, chunk(i, j, nt))),
            pl.BlockSpec((1, D_MODEL, fc), lambda i, j, e, nt, tv: (e[tile(i, nt)], 0, nj + chunk(i, j, nt))),
            pl.BlockSpec((1, fc, D_MODEL), lambda i, j, e, nt, tv: (e[tile(i, nt)], chunk(i, j, nt), 0)),
        ],
        out_specs=pl.BlockSpec((te, D_MODEL), lambda i, j, e, nt, tv: (tile(i, nt), 0)),
        scratch_shapes=[pltpu.VMEM((te, D_MODEL), BF16)],
    )
    return pl.pallas_call(
        _expert_kernel,
        grid_spec=grid_spec,
        out_shape=jax.ShapeDtypeStruct((n_slots, D_MODEL), F32),
        compiler_params=_cparams(("arbitrary", "arbitrary")),
        name="expert_swiglu",
    )(tile_expert, n_tiles, tile_valid, xs, wgu_bf, wgu_bf, wd_bf)


def _combine_kernel(s1_ref, s2_ref, rt_ref, x1_ref, g_ref, ys_ref, op_ref, os_ref, a_scr, b_scr, sem, *,
                    tiles_p):
    tm = TOKEN_TILE
    i = pl.program_id(0)

    def issue(t, c):
        _row_copy(ys_ref, s1_ref[t], a_scr, t, sem).start()
        _row_copy(ys_ref, s2_ref[t], b_scr, t, sem).start()
        return c

    lax.fori_loop(0, tm, issue, 0, unroll=ROW_DMA_UNROLL)

    rt = jnp.concatenate([rt_ref[...], jnp.zeros((LANES - SUBLANES, tm), F32)], axis=0)
    rt_t = rt.T
    g1 = rt_t[:, 4:5]
    g2 = rt_t[:, 5:6]

    pltpu.make_async_copy(ys_ref.at[pl.ds(0, tm)], a_scr, sem).wait()
    pltpu.make_async_copy(ys_ref.at[pl.ds(0, tm)], b_scr, sem).wait()

    x2 = x1_ref[...] + (g1 * a_scr[...] + g2 * b_scr[...])
    ms = jnp.mean(x2 * x2, axis=-1, keepdims=True)
    y = (x2 * lax.rsqrt(ms + EPS)) * g_ref[...]

    @pl.when(i < tiles_p)
    def _():
        op_ref[...] = y

    @pl.when(i >= tiles_p)
    def _():
        os_ref[...] = y


def _combine(slot1, slot2, rt, x1, g_final, ys, n_p):
    n = x1.shape[0]
    tm = TOKEN_TILE
    tiles_p = n_p // tm
    smem = lambda: pl.BlockSpec((tm,), lambda i: (i,), memory_space=pltpu.SMEM)
    return pl.pallas_call(
        functools.partial(_combine_kernel, tiles_p=tiles_p),
        grid=(n // tm,),
        in_specs=[
            smem(), smem(),
            pl.BlockSpec((SUBLANES, tm), lambda i: (0, i)),
            pl.BlockSpec((tm, D_MODEL), lambda i: (i, 0)),
            pl.BlockSpec((1, D_MODEL), lambda i: (0, 0)),
            pl.BlockSpec(memory_space=pl.ANY),
        ],
        out_specs=[
            pl.BlockSpec((tm, D_MODEL), lambda i: (jnp.minimum(i, tiles_p - 1), 0)),
            pl.BlockSpec((tm, D_MODEL), lambda i: (jnp.maximum(i - tiles_p, 0), 0)),
        ],
        out_shape=[
            jax.ShapeDtypeStruct((n_p, D_MODEL), F32),
            jax.ShapeDtypeStruct((n - n_p, D_MODEL), F32),
        ],
        scratch_shapes=[
            pltpu.VMEM((tm, D_MODEL), F32),
            pltpu.VMEM((tm, D_MODEL), F32),
            pltpu.SemaphoreType.DMA(()),
        ],
        compiler_params=_cparams(("arbitrary",)),
        name="moe_combine_norm",
    )(slot1, slot2, rt, x1, g_final, ys)


def _rope_tables(seq_max):
    half = ROT_DIM // 2
    inv = ROPE_THETA ** (-jnp.arange(0, ROT_DIM, 2, dtype=F32) / ROT_DIM)
    ang = jnp.arange(seq_max, dtype=F32)[:, None] * inv[None, :]
    cos, sin = jnp.cos(ang), jnp.sin(ang)
    ones = jnp.ones((seq_max, DIFF_HEAD_DIM - ROT_DIM), F32)
    zeros = jnp.zeros((seq_max, DIFF_HEAD_DIM - ROT_DIM), F32)
    z8 = jnp.zeros((seq_max, half), F32)
    ct = jnp.concatenate([cos, cos, ones], axis=1)
    sa = jnp.concatenate([-sin, z8, zeros], axis=1)
    sb = jnp.concatenate([z8, sin, zeros], axis=1)
    reps = 2 * LANES // DIFF_HEAD_DIM
    return tuple(jnp.tile(t, (1, reps)) for t in (ct, sa, sb))


def _channel_dft_matrix():
    c = np.arange(FOURIER_GROUP_DIM)
    ang = 2.0 * np.pi * ((c[:, None] * c[None, :]) % FOURIER_GROUP_DIM) / FOURIER_GROUP_DIM
    scale = FOURIER_GROUP_DIM ** -0.5
    eye = np.eye(FOURIER_GROUPS)
    bd = np.concatenate([np.kron(eye, np.cos(ang) * scale), np.kron(eye, np.sin(ang) * scale)], axis=1)
    return jnp.asarray(bd, dtype=BF16)


def _position_dft_matrices(seq):
    lo = FOURIER_GROUP_DIM
    hi = seq // lo
    sp = jnp.arange(seq, dtype=jnp.int32)[None, :]
    a = jnp.arange(hi, dtype=jnp.int32)[:, None]
    b = jnp.arange(lo, dtype=jnp.int32)[:, None]
    w = 2.0 * math.pi / seq
    ang_a = ((a * lo * sp) % seq).astype(F32) * w
    ang_b = ((b * sp) % seq).astype(F32) * w
    ca, sa = jnp.cos(ang_a)[:, None, :], jnp.sin(ang_a)[:, None, :]
    cb, sb = jnp.cos(ang_b)[None, :, :], jnp.sin(ang_b)[None, :, :]
    cmat = (ca * cb - sa * sb).reshape(seq, seq).astype(BF16)
    smat = (-(sa * cb + ca * sb)).reshape(seq, seq).astype(BF16)
    return cmat, smat


def kernel(x_prompt, x_sample, norm_mix, w_in, lambda_qk, subln_gain, w_out, norm_ffn, ffn_w_gate_up,
           ffn_w_down, router_w, expert_w_gate_up, expert_w_down, final_norm):
    nb_p, seq_p, _ = x_prompt.shape
    nb_s, seq_s, _ = x_sample.shape
    n_p, n_s = nb_p * seq_p, nb_s * seq_s
    n = n_p + n_s
    tm = TOKEN_TILE
    assert seq_p % tm == 0 and seq_s % tm == 0
    assert n_p % seq_s == 0 and seq_p % ATTN_K_TILE == 0 and seq_s % ATTN_K_TILE == 0
    assert n_p % min(seq_s, DFT_K_TILE) == 0
    assert DEPTH == 2

    ct, sa, sb = _rope_tables(max(seq_p, seq_s))
    bd = _channel_dft_matrix()
    dft_p = _position_dft_matrices(seq_p)
    dft_s = dft_p if seq_s == seq_p else _position_dft_matrices(seq_s)

    tiles_p = n_p // tm
    two_arrays = _Split(tiles_p, 0)
    one_array = _Split(tiles_p, tiles_p)
    xa, xb, x_split = x_prompt.reshape(n_p, D_MODEL), x_sample.reshape(n_s, D_MODEL), two_arrays
    out = None
    for layer in range(DEPTH):
        lam_init = 0.8 - 0.6 * math.exp(-0.3 * layer)
        yf, qkv = _norm_inproj(xa, xb, x_split, n, norm_mix[layer][None, :], w_in[layer].astype(BF16),
                               bd, ct, sa, sb, seq_p, seq_s)
        gain_col = subln_gain[layer][:, None]
        o_a = (_attention(qkv, lambda_qk[layer], gain_col, 0, nb_p, seq_p, lam_init),
               _attention(qkv, lambda_qk[layer], gain_col, n_p, nb_s, seq_s, lam_init))
        o_f = (_fourier(yf, dft_p[0], dft_p[1], 0, nb_p, seq_p),
               _fourier(yf, dft_s[0], dft_s[1], n_p, nb_s, seq_s))
        mixed = (xa, xb, x_split, o_f, o_a, two_arrays, n, w_out[layer].astype(BF16), norm_ffn[layer][None, :])
        if layer % 2 == 0:
            x = _outproj_ffn(*mixed, ffn_w_gate_up[layer // 2].astype(BF16), ffn_w_down[layer // 2].astype(BF16))
            xa, xb, x_split = x, x, one_array
        else:
            j = layer // 2
            x1, h, rt, cnt = _outproj_router(*mixed, router_w[j].T)
            te = EXPERT_TILE
            counts = cnt[:, 0].astype(jnp.int32)
            padded = ((counts + te - 1) // te) * te
            ends = jnp.cumsum(padded)
            offs = ends - padded
            e1, e2 = rt[0].astype(jnp.int32), rt[1].astype(jnp.int32)
            slot1 = offs[e1] + rt[2].astype(jnp.int32)
            slot2 = offs[e2] + rt[3].astype(jnp.int32)
            n_slots = 2 * n + N_EXPERTS * te
            starts = jnp.arange(n_slots // te, dtype=jnp.int32) * te
            tile_expert = jnp.minimum(jnp.sum((starts[:, None] >= ends[None, :]).astype(jnp.int32), axis=1),
                                      N_EXPERTS - 1)
            tile_valid = jnp.clip(counts[tile_expert] - (starts - offs[tile_expert]), 0, te).astype(jnp.int32)
            n_tiles = (ends[-1:] // te).astype(jnp.int32)
            xs = _dispatch(slot1, slot2, h, n_slots)
            ys = _experts(tile_expert, n_tiles, tile_valid, xs,
                          expert_w_gate_up[j].astype(BF16), expert_w_down[j].astype(BF16))
            out = _combine(slot1, slot2, rt, x1, final_norm[None, :], ys, n_p)
    y_p, y_s = out
    return (y_p.reshape(nb_p, seq_p, D_MODEL), y_s.reshape(nb_s, seq_s, D_MODEL))
```

```python
import functools
import math

import numpy as np
import jax
import jax.numpy as jnp
from jax import lax
from jax.experimental import pallas as pl
from jax.experimental.pallas import tpu as pltpu

F32 = jnp.float32
BF16 = jnp.bfloat16

D_MODEL = 1024
DEPTH = 2
FOURIER_WIDTH = 256
FOURIER_GROUPS = 4
FOURIER_GROUP_DIM = 64
ATTN_WIDTH = 768
DIFF_HEAD_DIM = 64
DIFF_V_DIM = 128
N_DIFF_HEADS = 6
QK_WIDTH = 768
IN_PROJ_WIDTH = 2560
ROT_DIM = 16
ROPE_THETA = 500000.0
FF_DENSE = 2816
N_EXPERTS = 8
FF_EXPERT = 3584
EPS = 1e-5

LANES = 128
SUBLANES = 8
VMEM_LIMIT_BYTES = 56 * 1024 * 1024

TOKEN_TILE = 512
ATTN_Q_TILE = 512
ATTN_K_TILE = 512
ATTN_K_SUPER = 2048
LOG2E = 1.4426950408889634
DFT_ROW_TILE = 1024
DFT_K_TILE = 2048
DFT_MAX_SEQS_PER_STEP = 4
FFN_CHUNK = 256
EXPERT_TILE = 1024
EXPERT_CHUNK = 1792
ROW_DMA_UNROLL = 8


def _cparams(sem):
    return pltpu.CompilerParams(dimension_semantics=sem, vmem_limit_bytes=VMEM_LIMIT_BYTES)


def _aligned(start, multiple):
    return start if isinstance(start, int) else pl.multiple_of(start, multiple)


def _silu_mul(g, u):
    return (g * (1.0 / (1.0 + jnp.exp(-g)))) * u


class _Split:
    def __init__(self, tiles_a, b_blk0):
        self.tiles_a, self.b_blk0 = tiles_a, b_blk0

    def specs(self, tm, width):
        ta, b0 = self.tiles_a, self.b_blk0
        return [pl.BlockSpec((tm, width), lambda i: (jnp.minimum(i, ta - 1), 0)),
                pl.BlockSpec((tm, width), lambda i: (b0 + jnp.maximum(i - ta, 0), 0))]

    def pick(self, a_ref, b_ref):
        return jnp.where(pl.program_id(0) < self.tiles_a, a_ref[...], b_ref[...])


def _norm_inproj_kernel(xa_ref, xb_ref, g_ref, w_ref, bd_ref, ct_ref, sa_ref, sb_ref, yf_ref, qkv_ref, h_scr,
                        *, split):
    x = split.pick(xa_ref, xb_ref)
    ms = jnp.mean(x * x, axis=-1, keepdims=True)
    h_scr[...] = ((x * lax.rsqrt(ms + EPS)) * g_ref[...]).astype(BF16)
    hb = h_scr[...]
    uf = jnp.dot(hb, w_ref[:, 0:FOURIER_WIDTH], preferred_element_type=F32)
    yf_ref[...] = jnp.dot(uf.astype(BF16), bd_ref[...], preferred_element_type=F32).astype(BF16)
    ct, sa, sb = ct_ref[...], sa_ref[...], sb_ref[...]
    width = 2 * LANES
    for blk in range(2 * QK_WIDTH // width):
        c0 = FOURIER_WIDTH + blk * width
        t = jnp.dot(hb, w_ref[:, c0:c0 + width], preferred_element_type=F32)
        r = t * ct + pltpu.roll(t, width - ROT_DIM // 2, 1) * sa + pltpu.roll(t, ROT_DIM // 2, 1) * sb
        if blk < QK_WIDTH // width:
            r = r * (DIFF_HEAD_DIM ** -0.5 * LOG2E)
        qkv_ref[:, blk * width:(blk + 1) * width] = r.astype(BF16)
    v0 = FOURIER_WIDTH + 2 * QK_WIDTH
    for blk in range(ATTN_WIDTH // width):
        t = jnp.dot(hb, w_ref[:, v0 + blk * width:v0 + (blk + 1) * width], preferred_element_type=F32)
        qkv_ref[:, 2 * QK_WIDTH + blk * width:2 * QK_WIDTH + (blk + 1) * width] = t.astype(BF16)


def _norm_inproj(xa, xb, split, n, g, w_bf, bd, ct, sa, sb, seq_p, seq_s):
    tm = TOKEN_TILE
    tiles_p = split.tiles_a
    per_p, per_s = seq_p // tm, seq_s // tm

    def pos_map(i):
        return (jnp.where(i < tiles_p, i % per_p, (i - tiles_p) % per_s), 0)

    tab = pl.BlockSpec((tm, 2 * LANES), pos_map)
    return pl.pallas_call(
        functools.partial(_norm_inproj_kernel, split=split),
        grid=(n // tm,),
        in_specs=split.specs(tm, D_MODEL) + [
            pl.BlockSpec((1, D_MODEL), lambda i: (0, 0)),
            pl.BlockSpec((D_MODEL, IN_PROJ_WIDTH), lambda i: (0, 0)),
            pl.BlockSpec((FOURIER_WIDTH, 2 * FOURIER_WIDTH), lambda i: (0, 0)),
            tab, tab, tab,
        ],
        out_specs=[
            pl.BlockSpec((tm, 2 * FOURIER_WIDTH), lambda i: (i, 0)),
            pl.BlockSpec((tm, 2 * QK_WIDTH + ATTN_WIDTH), lambda i: (i, 0)),
        ],
        out_shape=[
            jax.ShapeDtypeStruct((n, 2 * FOURIER_WIDTH), BF16),
            jax.ShapeDtypeStruct((n, 2 * QK_WIDTH + ATTN_WIDTH), BF16),
        ],
        scratch_shapes=[pltpu.VMEM((tm, D_MODEL), BF16)],
        compiler_params=_cparams(("arbitrary",)),
        name="norm_inproj",
    )(xa, xb, g, w_bf, bd, ct, sa, sb)


def _attn_kernel(lq_ref, q_ref, k_ref, v_ref, g_ref, o_ref, vt_scr, sa_scr, sb_scr, p_scr, acc_scr, *,
                 seq, lam_init):
    tq, tk = ATTN_Q_TILE, ATTN_K_TILE
    ks = min(seq, ATTN_K_SUPER)
    nsb, nkc = seq // ks, ks // tk
    groups = tk // SUBLANES
    n_units = (seq // tq) * nsb
    assert n_units % 2 == 0
    s_bufs = (sa_scr, sb_scr)
    for sb in range(nsb):
        for j in range(nkc):
            r0 = sb * ks + j * tk
            vt_scr[sb, :, j * tk:(j + 1) * tk] = v_ref[r0:r0 + tk, :].astype(F32).T.astype(BF16)

    lq = lq_ref[...]
    lam = (jnp.exp(jnp.sum(lq[0:1, :] * lq[1:2, :], axis=1, keepdims=True))
           - jnp.exp(jnp.sum(lq[2:3, :] * lq[3:4, :], axis=1, keepdims=True)) + lam_init)
    gain = g_ref[...]
    first = lax.broadcasted_iota(jnp.int32, (1, DIFF_V_DIM), 1) < DIFF_HEAD_DIM
    nt_dims = (((1,), (1,)), ((), ()))
    comps = (0, 1)

    def load_q(u):
        q = q_ref[pl.ds(_aligned((u // nsb) * tq, tq), tq), :]
        return (jnp.where(first, q, jnp.zeros_like(q)), jnp.where(first, jnp.zeros_like(q), q))

    def score_chunk(u, qc, s_scr, j, mx):
        kblk = k_ref[pl.ds(_aligned((u % nsb) * ks + j * tk, tk), tk), :]
        out = []
        for c in comps:
            s = lax.dot_general(kblk, qc[c], nt_dims, preferred_element_type=F32)
            s_scr[c, j] = s
            out.append(jnp.maximum(mx[c], jnp.max(s.reshape(groups, SUBLANES, tq), axis=0)))
        return out

    def exp_chunk(s_scr, j, m_new, ls):
        out = []
        for c in comps:
            p = jnp.exp2(s_scr[c, j] - m_new[c])
            out.append(ls[c] + jnp.sum(p.reshape(groups, SUBLANES, tq), axis=0))
            p_scr[c, j * tk:(j + 1) * tk, :] = p.astype(BF16)
        return out

    def unit(u, slot, st, with_next):
        mx, m, l = st[0:2], st[2:4], st[4:6]
        sb, qi = u % nsb, u // nsb
        m_new = [jnp.maximum(m[c], jnp.max(mx[c], axis=0, keepdims=True)) for c in comps]
        alpha = [jnp.exp2(m[c] - m_new[c]) for c in comps]
        ls = [jnp.zeros((SUBLANES, tq), F32) for _ in comps]
        mx_next = [jnp.full((SUBLANES, tq), -jnp.inf, F32) for _ in comps]
        qc_next = load_q(u + 1) if with_next else None
        for j in range(nkc):
            if with_next:
                mx_next = score_chunk(u + 1, qc_next, s_bufs[1 - slot], j, mx_next)
            ls = exp_chunk(s_bufs[slot], j, m_new, ls)
        l_new = [alpha[c] * l[c] + jnp.sum(ls[c], axis=0, keepdims=True) for c in comps]
        vt = vt_scr[sb]
        for c in comps:
            acc_scr[c] = alpha[c] * acc_scr[c] + jnp.dot(vt, p_scr[c], preferred_element_type=F32)

        last = jnp.asarray(sb == nsb - 1)

        @pl.when(last)
        def _():
            o_t = acc_scr[0] * (1.0 / l_new[0]) - lam * (acc_scr[1] * (1.0 / l_new[1]))
            ms = jnp.mean(o_t * o_t, axis=0, keepdims=True)
            y_t = ((o_t * lax.rsqrt(ms + EPS)) * gain) * (1.0 - lam_init)
            o_ref[pl.ds(_aligned(qi * tq, tq), tq), :] = y_t.T.astype(BF16)
            acc_scr[...] = jnp.zeros_like(acc_scr)

        m_out = [jnp.where(last, -jnp.inf, m_new[c]) for c in comps]
        l_out = [jnp.where(last, 0.0, l_new[c]) for c in comps]
        return (mx_next[0], mx_next[1], m_out[0], m_out[1], l_out[0], l_out[1])

    acc_scr[...] = jnp.zeros_like(acc_scr)
    mx0 = [jnp.full((SUBLANES, tq), -jnp.inf, F32) for _ in comps]
    qc0 = load_q(0)
    for j in range(nkc):
        mx0 = score_chunk(0, qc0, s_bufs[0], j, mx0)
    neg = jnp.full((1, tq), -jnp.inf, F32)
    zero = jnp.zeros((1, tq), F32)
    st = (mx0[0], mx0[1], neg, neg, zero, zero)

    def pair(i, st):
        st = unit(2 * i, 0, st, True)
        return unit(2 * i + 1, 1, st, True)

    st = lax.fori_loop(0, n_units // 2 - 1, pair, st)
    st = unit(n_units - 2, 0, st, True)
    unit(n_units - 1, 1, st, False)


def _attention(qkv, lq, gain_col, row0, nb, seq, lam_init):
    blk0 = row0 // seq
    nh = N_DIFF_HEADS
    tq, tk = ATTN_Q_TILE, ATTN_K_TILE
    ks = min(seq, ATTN_K_SUPER)
    kern = functools.partial(_attn_kernel, seq=seq, lam_init=lam_init)
    return pl.pallas_call(
        kern,
        grid=(nb, nh),
        in_specs=[
            pl.BlockSpec((4, DIFF_HEAD_DIM), lambda b, h: (0, 0)),
            pl.BlockSpec((seq, LANES), lambda b, h: (blk0 + b, h)),
            pl.BlockSpec((seq, LANES), lambda b, h: (blk0 + b, nh + h)),
            pl.BlockSpec((seq, LANES), lambda b, h: (blk0 + b, 2 * nh + h)),
            pl.BlockSpec((DIFF_V_DIM, 1), lambda b, h: (0, 0)),
        ],
        out_specs=pl.BlockSpec((seq, LANES), lambda b, h: (b, h)),
        out_shape=jax.ShapeDtypeStruct((nb * seq, ATTN_WIDTH), BF16),
        scratch_shapes=[
            pltpu.VMEM((seq // ks, DIFF_V_DIM, ks), BF16),
            pltpu.VMEM((2, ks // tk, tk, tq), F32),
            pltpu.VMEM((2, ks // tk, tk, tq), F32),
            pltpu.VMEM((2, ks, tq), BF16),
            pltpu.VMEM((2, DIFF_V_DIM, tq), F32),
        ],
        compiler_params=_cparams(("arbitrary", "arbitrary")),
        name="diff_attention",
    )(lq, qkv, qkv, qkv, gain_col)


def _dft_kernel(c_ref, s_ref, *rest, nk, nbb, scale):
    y_refs, o_ref, acc_scr = rest[:nbb], rest[nbb], rest[nbb + 1]
    k = pl.program_id(2)

    @pl.when(k == 0)
    def _():
        acc_scr[...] = jnp.zeros_like(acc_scr)

    c, s = c_ref[...], s_ref[...]
    for bb in range(nbb):
        y = y_refs[bb][...]
        acc_scr[bb] += (jnp.dot(c, y[:, 0:FOURIER_WIDTH], preferred_element_type=F32)
                        + jnp.dot(s, y[:, FOURIER_WIDTH:2 * FOURIER_WIDTH], preferred_element_type=F32))

    @pl.when(k == nk - 1)
    def _():
        o_ref[...] = (acc_scr[...] * scale).astype(BF16)


def _fourier(yf, cmat, smat, row0, nb, seq):
    tm = min(seq, DFT_ROW_TILE)
    tk = min(seq, DFT_K_TILE)
    ni, nk = seq // tm, seq // tk
    nbb = nb if (nk > 1 and nb <= DFT_MAX_SEQS_PER_STEP) else 1
    yblk0 = row0 // tk

    def y_spec(bb):
        return pl.BlockSpec((tk, 2 * FOURIER_WIDTH), lambda i, g, k: (yblk0 + (g * nbb + bb) * nk + k, 0))

    kern = functools.partial(_dft_kernel, nk=nk, nbb=nbb, scale=1.0 / math.sqrt(seq))
    out = pl.pallas_call(
        kern,
        grid=(ni, nb // nbb, nk),
        in_specs=[
            pl.BlockSpec((tm, tk), lambda i, g, k: (i, k)),
            pl.BlockSpec((tm, tk), lambda i, g, k: (i, k)),
        ] + [y_spec(bb) for bb in range(nbb)],
        out_specs=pl.BlockSpec((nbb, tm, FOURIER_WIDTH), lambda i, g, k: (g, i, 0)),
        out_shape=jax.ShapeDtypeStruct((nb, seq, FOURIER_WIDTH), BF16),
        scratch_shapes=[pltpu.VMEM((nbb, tm, FOURIER_WIDTH), F32)],
        compiler_params=_cparams(("arbitrary", "arbitrary", "arbitrary")),
        name="fourier_dft",
    )(cmat, smat, *([yf] * nbb))
    return out.reshape(nb * seq, FOURIER_WIDTH)


def _mixed_residual(xa_ref, xb_ref, ofa_ref, ofb_ref, oaa_ref, oab_ref, w_ref, g_ref, x_split, o_split):
    y = (jnp.dot(o_split.pick(ofa_ref, ofb_ref), w_ref[0:FOURIER_WIDTH, :], preferred_element_type=F32)
         + jnp.dot(o_split.pick(oaa_ref, oab_ref), w_ref[FOURIER_WIDTH:D_MODEL, :], preferred_element_type=F32))
    x1 = x_split.pick(xa_ref, xb_ref) + y
    ms = jnp.mean(x1 * x1, axis=-1, keepdims=True)
    return x1, (x1 * lax.rsqrt(ms + EPS)) * g_ref[...]


def _outproj_ffn_kernel(xa_ref, xb_ref, ofa_ref, ofb_ref, oaa_ref, oab_ref, w_ref, g_ref, wgu_ref, wd_ref,
                        o_ref, h_scr, *, x_split, o_split):
    x1, h = _mixed_residual(xa_ref, xb_ref, ofa_ref, ofb_ref, oaa_ref, oab_ref, w_ref, g_ref, x_split, o_split)
    o_ref[...] = x1
    h_scr[...] = h.astype(BF16)
    hb = h_scr[...]
    fc = FFN_CHUNK
    for j in range(FF_DENSE // fc):
        g = jnp.dot(hb, wgu_ref[:, j * fc:(j + 1) * fc], preferred_element_type=F32)
        u = jnp.dot(hb, wgu_ref[:, FF_DENSE + j * fc:FF_DENSE + (j + 1) * fc], preferred_element_type=F32)
        o_ref[...] += jnp.dot(_silu_mul(g, u).astype(BF16), wd_ref[j * fc:(j + 1) * fc, :],
                              preferred_element_type=F32)


def _outproj_router_kernel(xa_ref, xb_ref, ofa_ref, ofb_ref, oaa_ref, oab_ref, w_ref, g_ref, rw_ref,
                           x1_ref, h_ref, rt_ref, cnt_ref, carry_scr, *, x_split, o_split):
    x1, h = _mixed_residual(xa_ref, xb_ref, ofa_ref, ofb_ref, oaa_ref, oab_ref, w_ref, g_ref, x_split, o_split)
    x1_ref[...] = x1
    h_ref[...] = h.astype(BF16)
    _route(h, rw_ref, rt_ref, cnt_ref, carry_scr)


def _resident(shape):
    return pl.BlockSpec(shape, lambda i: (0,) * len(shape), pipeline_mode=pl.Buffered(1))


def _mixed_specs(x_split, o_split, tm):
    return (x_split.specs(tm, D_MODEL) + o_split.specs(tm, FOURIER_WIDTH) + o_split.specs(tm, ATTN_WIDTH)
            + [_resident((D_MODEL, D_MODEL)), pl.BlockSpec((1, D_MODEL), lambda i: (0, 0))])


def _outproj_ffn(xa, xb, x_split, o_f, o_a, o_split, n, w_bf, g, wgu_bf, wd_bf):
    tm = TOKEN_TILE
    return pl.pallas_call(
        functools.partial(_outproj_ffn_kernel, x_split=x_split, o_split=o_split),
        grid=(n // tm,),
        in_specs=_mixed_specs(x_split, o_split, tm) + [_resident((D_MODEL, 2 * FF_DENSE)),
                                                      _resident((FF_DENSE, D_MODEL))],
        out_specs=pl.BlockSpec((tm, D_MODEL), lambda i: (i, 0)),
        out_shape=jax.ShapeDtypeStruct((n, D_MODEL), F32),
        scratch_shapes=[pltpu.VMEM((tm, D_MODEL), BF16)],
        compiler_params=_cparams(("arbitrary",)),
        name="outproj_dense_swiglu",
    )(xa, xb, o_f[0], o_f[1], o_a[0], o_a[1], w_bf, g, wgu_bf, wd_bf)


def _outproj_router(xa, xb, x_split, o_f, o_a, o_split, n, w_bf, g, rw_t):
    tm = TOKEN_TILE
    return pl.pallas_call(
        functools.partial(_outproj_router_kernel, x_split=x_split, o_split=o_split),
        grid=(n // tm,),
        in_specs=_mixed_specs(x_split, o_split, tm) + [pl.BlockSpec((N_EXPERTS, D_MODEL), lambda i: (0, 0))],
        out_specs=[
            pl.BlockSpec((tm, D_MODEL), lambda i: (i, 0)),
            pl.BlockSpec((tm, D_MODEL), lambda i: (i, 0)),
            pl.BlockSpec((SUBLANES, tm), lambda i: (0, i)),
            pl.BlockSpec((N_EXPERTS, LANES), lambda i: (0, 0)),
        ],
        out_shape=[
            jax.ShapeDtypeStruct((n, D_MODEL), F32),
            jax.ShapeDtypeStruct((n, D_MODEL), BF16),
            jax.ShapeDtypeStruct((SUBLANES, n), F32),
            jax.ShapeDtypeStruct((N_EXPERTS, LANES), F32),
        ],
        scratch_shapes=[pltpu.VMEM((N_EXPERTS, LANES), F32)],
        compiler_params=_cparams(("arbitrary",)),
        name="outproj_router",
    )(xa, xb, o_f[0], o_f[1], o_a[0], o_a[1], w_bf, g, rw_t)


def _route(h, rw_ref, rt_ref, cnt_ref, carry_scr):
    tm = TOKEN_TILE

    @pl.when(pl.program_id(0) == 0)
    def _():
        carry_scr[...] = jnp.zeros_like(carry_scr)

    h_hi = h.astype(BF16)
    h_lo = (h - h_hi.astype(F32)).astype(BF16)
    rw = rw_ref[...]
    rw_hi = rw.astype(BF16)
    rw_lo = (rw - rw_hi.astype(F32)).astype(BF16)
    nt_dims = (((1,), (1,)), ((), ()))
    dg = lambda a, b: lax.dot_general(a, b, nt_dims, preferred_element_type=F32)
    logits = dg(rw_hi, h_hi) + (dg(rw_hi, h_lo) + dg(rw_lo, h_hi))

    eidx = lax.broadcasted_iota(jnp.int32, (N_EXPERTS, tm), 0)
    m1 = jnp.max(logits, axis=0, keepdims=True)
    i1 = jnp.min(jnp.where(logits == m1, eidx, N_EXPERTS), axis=0, keepdims=True)
    oh1 = eidx == i1
    rest = jnp.where(oh1, -jnp.inf, logits)
    m2 = jnp.max(rest, axis=0, keepdims=True)
    i2 = jnp.min(jnp.where(rest == m2, eidx, N_EXPERTS), axis=0, keepdims=True)
    oh2 = eidx == i2
    e = jnp.exp(m2 - m1)
    g1 = 1.0 / (1.0 + e)
    g2 = e / (1.0 + e)

    onehot = jnp.where(oh1 | oh2, 1.0, 0.0)
    upper = (lax.broadcasted_iota(jnp.int32, (tm, tm), 0)
             < lax.broadcasted_iota(jnp.int32, (tm, tm), 1))
    before = jnp.dot(onehot.astype(BF16), jnp.where(upper, 1.0, 0.0).astype(BF16),
                     preferred_element_type=F32)
    rank = before + carry_scr[:, 0:1]
    r1 = jnp.sum(jnp.where(oh1, rank, 0.0), axis=0, keepdims=True)
    r2 = jnp.sum(jnp.where(oh2, rank, 0.0), axis=0, keepdims=True)
    zero = jnp.zeros_like(g1)
    rt_ref[...] = jnp.concatenate(
        [i1.astype(F32), i2.astype(F32), r1, r2, g1, g2, zero, zero], axis=0)
    total = carry_scr[...] + jnp.sum(onehot, axis=1, keepdims=True)
    carry_scr[...] = total
    cnt_ref[...] = total


def _row_copy(src, src_row, dst, dst_row, sem):
    return pltpu.make_async_copy(src.at[pl.ds(src_row, 1)], dst.at[pl.ds(dst_row, 1)], sem)


def _dispatch_kernel(s1_ref, s2_ref, h_ref, xs_ref, rows_scr, sem):
    tm = TOKEN_TILE
    rows_scr[...] = h_ref[...].astype(F32)

    def issue(t, c):
        _row_copy(rows_scr, t, xs_ref, s1_ref[t], sem).start()
        _row_copy(rows_scr, t, xs_ref, s2_ref[t], sem).start()
        return c

    lax.fori_loop(0, tm, issue, 0, unroll=ROW_DMA_UNROLL)
    for _ in range(2):
        pltpu.make_async_copy(rows_scr, xs_ref.at[pl.ds(0, tm)], sem).wait()


def _dispatch(slot1, slot2, h, n_slots):
    n = h.shape[0]
    tm = TOKEN_TILE
    smem = lambda: pl.BlockSpec((tm,), lambda i: (i,), memory_space=pltpu.SMEM)
    return pl.pallas_call(
        _dispatch_kernel,
        grid=(n // tm,),
        in_specs=[smem(), smem(), pl.BlockSpec((tm, D_MODEL), lambda i: (i, 0))],
        out_specs=pl.BlockSpec(memory_space=pl.ANY),
        out_shape=jax.ShapeDtypeStruct((n_slots, D_MODEL), F32),
        scratch_shapes=[pltpu.VMEM((tm, D_MODEL), F32), pltpu.SemaphoreType.DMA(())],
        compiler_params=_cparams(("arbitrary",)),
        name="moe_dispatch",
    )(slot1, slot2, h)


def _expert_kernel(te_ref, nt_ref, tv_ref, xs_ref, wg_ref, wu_ref, wd_ref, ys_ref, xb_scr):
    i = pl.program_id(0)
    j = pl.program_id(1)

    @pl.when(i < nt_ref[0])
    def _():
        @pl.when(j == 0)
        def _():
            rows = lax.broadcasted_iota(jnp.int32, (EXPERT_TILE, 1), 0)
            xb_scr[...] = jnp.where(rows < tv_ref[i], xs_ref[...], 0.0).astype(BF16)
            ys_ref[...] = jnp.zeros_like(ys_ref)

        xb = xb_scr[...]
        fc = FFN_CHUNK
        for t in range(EXPERT_CHUNK // fc):
            g = jnp.dot(xb, wg_ref[0, :, t * fc:(t + 1) * fc], preferred_element_type=F32)
            u = jnp.dot(xb, wu_ref[0, :, t * fc:(t + 1) * fc], preferred_element_type=F32)
            ys_ref[...] += jnp.dot(_silu_mul(g, u).astype(BF16), wd_ref[0, t * fc:(t + 1) * fc, :],
                                   preferred_element_type=F32)


def _experts(tile_expert, n_tiles, tile_valid, xs, wgu_bf, wd_bf):
    n_slots = xs.shape[0]
    te, fc = EXPERT_TILE, EXPERT_CHUNK
    nj = FF_EXPERT // fc
    max_tiles = n_slots // te

    def tile(i, nt):
        return jnp.minimum(i, nt[0] - 1)

    def chunk(i, j, nt):
        return jnp.where(i < nt[0], j, nj - 1)

    grid_spec = pltpu.PrefetchScalarGridSpec(
        num_scalar_prefetch=3,
        grid=(max_tiles, nj),
        in_specs=[
            pl.BlockSpec((te, D_MODEL), lambda i, j, e, nt, tv: (tile(i, nt), 0)),
            pl.BlockSpec((1, D_MODEL, fc), lambda i, j, e, nt, tv: (e[tile(i, nt)], 0, chunk(i, j, nt))),
            pl.BlockSpec((1, D_MODEL, fc), lambda i, j, e, nt, tv: (e[tile(i, nt)], 0, nj + chunk(i, j, nt))),
            pl.BlockSpec((1, fc, D_MODEL), lambda i, j, e, nt, tv: (e[tile(i, nt)], chunk(i, j, nt), 0)),
        ],
        out_specs=pl.BlockSpec((te, D_MODEL), lambda i, j, e, nt, tv: (tile(i, nt), 0)),
        scratch_shapes=[pltpu.VMEM((te, D_MODEL), BF16)],
    )
    return pl.pallas_call(
        _expert_kernel,
        grid_spec=grid_spec,
        out_shape=jax.ShapeDtypeStruct((n_slots, D_MODEL), F32),
        compiler_params=_cparams(("arbitrary", "arbitrary")),
        name="expert_swiglu",
    )(tile_expert, n_tiles, tile_valid, xs, wgu_bf, wgu_bf, wd_bf)


def _combine_kernel(s1_ref, s2_ref, rt_ref, x1_ref, g_ref, ys_ref, op_ref, os_ref, a_scr, b_scr, y_scr, sems, *,
                    tiles_p):
    tm = TOKEN_TILE
    half = tm // 2
    i = pl.program_id(0)

    for hh in range(2):
        def issue(t, c, sem=sems.at[hh]):
            _row_copy(ys_ref, s1_ref[t], a_scr, t, sem).start()
            _row_copy(ys_ref, s2_ref[t], b_scr, t, sem).start()
            return c

        lax.fori_loop(hh * half, (hh + 1) * half, issue, 0, unroll=ROW_DMA_UNROLL)

    rt = jnp.concatenate([rt_ref[...], jnp.zeros((LANES - SUBLANES, tm), F32)], axis=0)
    rt_t = rt.T

    for hh in range(2):
        rows = pl.ds(hh * half, half)
        pltpu.make_async_copy(ys_ref.at[pl.ds(0, half)], a_scr.at[rows], sems.at[hh]).wait()
        pltpu.make_async_copy(ys_ref.at[pl.ds(0, half)], b_scr.at[rows], sems.at[hh]).wait()
        g1 = rt_t[hh * half:(hh + 1) * half, 4:5]
        g2 = rt_t[hh * half:(hh + 1) * half, 5:6]
        x2 = x1_ref[rows, :] + (g1 * a_scr[rows, :] + g2 * b_scr[rows, :])
        ms = jnp.mean(x2 * x2, axis=-1, keepdims=True)
        y_scr[rows, :] = (x2 * lax.rsqrt(ms + EPS)) * g_ref[...]

    @pl.when(i < tiles_p)
    def _():
        op_ref[...] = y_scr[...]

    @pl.when(i >= tiles_p)
    def _():
        os_ref[...] = y_scr[...]


def _combine(slot1, slot2, rt, x1, g_final, ys, n_p):
    n = x1.shape[0]
    tm = TOKEN_TILE
    tiles_p = n_p // tm
    smem = lambda: pl.BlockSpec((tm,), lambda i: (i,), memory_space=pltpu.SMEM)
    return pl.pallas_call(
        functools.partial(_combine_kernel, tiles_p=tiles_p),
        grid=(n // tm,),
        in_specs=[
            smem(), smem(),
            pl.BlockSpec((SUBLANES, tm), lambda i: (0, i)),
            pl.BlockSpec((tm, D_MODEL), lambda i: (i, 0)),
            pl.BlockSpec((1, D_MODEL), lambda i: (0, 0)),
            pl.BlockSpec(memory_space=pl.ANY),
        ],
        out_specs=[
            pl.BlockSpec((tm, D_MODEL), lambda i: (jnp.minimum(i, tiles_p - 1), 0)),
            pl.BlockSpec((tm, D_MODEL), lambda i: (jnp.maximum(i - tiles_p, 0), 0)),
        ],
        out_shape=[
            jax.ShapeDtypeStruct((n_p, D_MODEL), F32),
            jax.ShapeDtypeStruct((n - n_p, D_MODEL), F32),
        ],
        scratch_shapes=[
            pltpu.VMEM((tm, D_MODEL), F32),
            pltpu.VMEM((tm, D_MODEL), F32),
            pltpu.VMEM((tm, D_MODEL), F32),
            pltpu.SemaphoreType.DMA((2,)),
        ],
        compiler_params=_cparams(("arbitrary",)),
        name="moe_combine_norm",
    )(slot1, slot2, rt, x1, g_final, ys)


def _rope_tables(seq_max):
    half = ROT_DIM // 2
    inv = ROPE_THETA ** (-jnp.arange(0, ROT_DIM, 2, dtype=F32) / ROT_DIM)
    ang = jnp.arange(seq_max, dtype=F32)[:, None] * inv[None, :]
    cos, sin = jnp.cos(ang), jnp.sin(ang)
    ones = jnp.ones((seq_max, DIFF_HEAD_DIM - ROT_DIM), F32)
    zeros = jnp.zeros((seq_max, DIFF_HEAD_DIM - ROT_DIM), F32)
    z8 = jnp.zeros((seq_max, half), F32)
    ct = jnp.concatenate([cos, cos, ones], axis=1)
    sa = jnp.concatenate([-sin, z8, zeros], axis=1)
    sb = jnp.concatenate([z8, sin, zeros], axis=1)
    reps = 2 * LANES // DIFF_HEAD_DIM
    return tuple(jnp.tile(t, (1, reps)) for t in (ct, sa, sb))


def _channel_dft_matrix():
    c = np.arange(FOURIER_GROUP_DIM)
    ang = 2.0 * np.pi * ((c[:, None] * c[None, :]) % FOURIER_GROUP_DIM) / FOURIER_GROUP_DIM
    scale = FOURIER_GROUP_DIM ** -0.5
    eye = np.eye(FOURIER_GROUPS)
    bd = np.concatenate([np.kron(eye, np.cos(ang) * scale), np.kron(eye, np.sin(ang) * scale)], axis=1)
    return jnp.asarray(bd, dtype=BF16)


def _position_dft_matrices(seq):
    lo = FOURIER_GROUP_DIM
    hi = seq // lo
    sp = jnp.arange(seq, dtype=jnp.int32)[None, :]
    a = jnp.arange(hi, dtype=jnp.int32)[:, None]
    b = jnp.arange(lo, dtype=jnp.int32)[:, None]
    w = 2.0 * math.pi / seq
    ang_a = ((a * lo * sp) % seq).astype(F32) * w
    ang_b = ((b * sp) % seq).astype(F32) * w
    ca, sa = jnp.cos(ang_a)[:, None, :], jnp.sin(ang_a)[:, None, :]
    cb, sb = jnp.cos(ang_b)[None, :, :], jnp.sin(ang_b)[None, :, :]
    cmat = (ca * cb - sa * sb).reshape(seq, seq).astype(BF16)
    smat = (-(sa * cb + ca * sb)).reshape(seq, seq).astype(BF16)
    return cmat, smat


def kernel(x_prompt, x_sample, norm_mix, w_in, lambda_qk, subln_gain, w_out, norm_ffn, ffn_w_gate_up,
           ffn_w_down, router_w, expert_w_gate_up, expert_w_down, final_norm):
    nb_p, seq_p, _ = x_prompt.shape
    nb_s, seq_s, _ = x_sample.shape
    n_p, n_s = nb_p * seq_p, nb_s * seq_s
    n = n_p + n_s
    tm = TOKEN_TILE
    assert seq_p % tm == 0 and seq_s % tm == 0
    assert n_p % seq_s == 0 and seq_p % ATTN_K_TILE == 0 and seq_s % ATTN_K_TILE == 0
    assert n_p % min(seq_s, DFT_K_TILE) == 0
    assert DEPTH == 2

    ct, sa, sb = _rope_tables(max(seq_p, seq_s))
    bd = _channel_dft_matrix()
    dft_p = _position_dft_matrices(seq_p)
    dft_s = dft_p if seq_s == seq_p else _position_dft_matrices(seq_s)

    tiles_p = n_p // tm
    two_arrays = _Split(tiles_p, 0)
    one_array = _Split(tiles_p, tiles_p)
    xa, xb, x_split = x_prompt.reshape(n_p, D_MODEL), x_sample.reshape(n_s, D_MODEL), two_arrays
    out = None
    for layer in range(DEPTH):
        lam_init = 0.8 - 0.6 * math.exp(-0.3 * layer)
        yf, qkv = _norm_inproj(xa, xb, x_split, n, norm_mix[layer][None, :], w_in[layer].astype(BF16),
                               bd, ct, sa, sb, seq_p, seq_s)
        gain_col = subln_gain[layer][:, None]
        o_a = (_attention(qkv, lambda_qk[layer], gain_col, 0, nb_p, seq_p, lam_init),
               _attention(qkv, lambda_qk[layer], gain_col, n_p, nb_s, seq_s, lam_init))
        o_f = (_fourier(yf, dft_p[0], dft_p[1], 0, nb_p, seq_p),
               _fourier(yf, dft_s[0], dft_s[1], n_p, nb_s, seq_s))
        mixed = (xa, xb, x_split, o_f, o_a, two_arrays, n, w_out[layer].astype(BF16), norm_ffn[layer][None, :])
        if layer % 2 == 0:
            x = _outproj_ffn(*mixed, ffn_w_gate_up[layer // 2].astype(BF16), ffn_w_down[layer // 2].astype(BF16))
            xa, xb, x_split = x, x, one_array
        else:
            j = layer // 2
            x1, h, rt, cnt = _outproj_router(*mixed, router_w[j].T)
            te = EXPERT_TILE
            counts = cnt[:, 0].astype(jnp.int32)
            padded = ((counts + te - 1) // te) * te
            ends = jnp.cumsum(padded)
            offs = ends - padded
            e1, e2 = rt[0].astype(jnp.int32), rt[1].astype(jnp.int32)
            slot1 = offs[e1] + rt[2].astype(jnp.int32)
            slot2 = offs[e2] + rt[3].astype(jnp.int32)
            n_slots = 2 * n + N_EXPERTS * te
            starts = jnp.arange(n_slots // te, dtype=jnp.int32) * te
            tile_expert = jnp.minimum(jnp.sum((starts[:, None] >= ends[None, :]).astype(jnp.int32), axis=1),
                                      N_EXPERTS - 1)
            tile_valid = jnp.clip(counts[tile_expert] - (starts - offs[tile_expert]), 0, te).astype(jnp.int32)
            n_tiles = (ends[-1:] // te).astype(jnp.int32)
            xs = _dispatch(slot1, slot2, h, n_slots)
            ys = _experts(tile_expert, n_tiles, tile_valid, xs,
                          expert_w_gate_up[j].astype(BF16), expert_w_down[j].astype(BF16))
            out = _combine(slot1, slot2, rt, x1, final_norm[None, :], ys, n_p)
    y_p, y_s = out
    return (y_p.reshape(nb_p, seq_p, D_MODEL), y_s.reshape(nb_s, seq_s, D_MODEL))
```

```python
import functools
import math

import numpy as np
import jax
import jax.numpy as jnp
from jax import lax
from jax.experimental import pallas as pl
from jax.experimental.pallas import tpu as pltpu

F32 = jnp.float32
BF16 = jnp.bfloat16

D_MODEL = 1024
DEPTH = 2
FOURIER_WIDTH = 256
FOURIER_GROUPS = 4
FOURIER_GROUP_DIM = 64
ATTN_WIDTH = 768
DIFF_HEAD_DIM = 64
DIFF_V_DIM = 128
N_DIFF_HEADS = 6
QK_WIDTH = 768
IN_PROJ_WIDTH = 2560
ROT_DIM = 16
ROPE_THETA = 500000.0
FF_DENSE = 2816
N_EXPERTS = 8
FF_EXPERT = 3584
EPS = 1e-5

LANES = 128
SUBLANES = 8
VMEM_LIMIT_BYTES = 56 * 1024 * 1024

TOKEN_TILE = 512
ATTN_Q_TILE = 512
ATTN_K_TILE = 512
ATTN_K_SUPER = 2048
LOG2E = 1.4426950408889634
DFT_ROW_TILE = 1024
DFT_K_TILE = 2048
DFT_MAX_SEQS_PER_STEP = 4
FFN_CHUNK = 256
EXPERT_TILE = 1024
EXPERT_CHUNK = 1792
ROW_DMA_UNROLL = 8


def _cparams(sem):
    return pltpu.CompilerParams(dimension_semantics=sem, vmem_limit_bytes=VMEM_LIMIT_BYTES)


def _aligned(start, multiple):
    return start if isinstance(start, int) else pl.multiple_of(start, multiple)


def _silu_mul(g, u):
    return (g * (1.0 / (1.0 + jnp.exp(-g)))) * u


class _Split:
    def __init__(self, tiles_a, b_blk0):
        self.tiles_a, self.b_blk0 = tiles_a, b_blk0

    def specs(self, tm, width):
        ta, b0 = self.tiles_a, self.b_blk0
        return [pl.BlockSpec((tm, width), lambda i: (jnp.minimum(i, ta - 1), 0)),
                pl.BlockSpec((tm, width), lambda i: (b0 + jnp.maximum(i - ta, 0), 0))]

    def pick(self, a_ref, b_ref):
        return jnp.where(pl.program_id(0) < self.tiles_a, a_ref[...], b_ref[...])


def _norm_inproj_kernel(xa_ref, xb_ref, g_ref, w_ref, bd_ref, ct_ref, sa_ref, sb_ref, yf_ref, qkv_ref, h_scr,
                        *, split):
    x = split.pick(xa_ref, xb_ref)
    ms = jnp.mean(x * x, axis=-1, keepdims=True)
    h_scr[...] = ((x * lax.rsqrt(ms + EPS)) * g_ref[...]).astype(BF16)
    hb = h_scr[...]
    uf = jnp.dot(hb, w_ref[:, 0:FOURIER_WIDTH], preferred_element_type=F32)
    yf_ref[...] = jnp.dot(uf.astype(BF16), bd_ref[...], preferred_element_type=F32).astype(BF16)
    ct, sa, sb = ct_ref[...], sa_ref[...], sb_ref[...]
    width = 2 * LANES
    for blk in range(2 * QK_WIDTH // width):
        c0 = FOURIER_WIDTH + blk * width
        t = jnp.dot(hb, w_ref[:, c0:c0 + width], preferred_element_type=F32)
        r = t * ct + pltpu.roll(t, width - ROT_DIM // 2, 1) * sa + pltpu.roll(t, ROT_DIM // 2, 1) * sb
        if blk < QK_WIDTH // width:
            r = r * (DIFF_HEAD_DIM ** -0.5 * LOG2E)
        qkv_ref[:, blk * width:(blk + 1) * width] = r.astype(BF16)
    v0 = FOURIER_WIDTH + 2 * QK_WIDTH
    for blk in range(ATTN_WIDTH // width):
        t = jnp.dot(hb, w_ref[:, v0 + blk * width:v0 + (blk + 1) * width], preferred_element_type=F32)
        qkv_ref[:, 2 * QK_WIDTH + blk * width:2 * QK_WIDTH + (blk + 1) * width] = t.astype(BF16)


def _norm_inproj(xa, xb, split, n, g, w_bf, bd, ct, sa, sb, seq_p, seq_s):
    tm = TOKEN_TILE
    tiles_p = split.tiles_a
    per_p, per_s = seq_p // tm, seq_s // tm

    def pos_map(i):
        return (jnp.where(i < tiles_p, i % per_p, (i - tiles_p) % per_s), 0)

    tab = pl.BlockSpec((tm, 2 * LANES), pos_map)
    return pl.pallas_call(
        functools.partial(_norm_inproj_kernel, split=split),
        grid=(n // tm,),
        in_specs=split.specs(tm, D_MODEL) + [
            pl.BlockSpec((1, D_MODEL), lambda i: (0, 0)),
            pl.BlockSpec((D_MODEL, IN_PROJ_WIDTH), lambda i: (0, 0)),
            pl.BlockSpec((FOURIER_WIDTH, 2 * FOURIER_WIDTH), lambda i: (0, 0)),
            tab, tab, tab,
        ],
        out_specs=[
            pl.BlockSpec((tm, 2 * FOURIER_WIDTH), lambda i: (i, 0)),
            pl.BlockSpec((tm, 2 * QK_WIDTH + ATTN_WIDTH), lambda i: (i, 0)),
        ],
        out_shape=[
            jax.ShapeDtypeStruct((n, 2 * FOURIER_WIDTH), BF16),
            jax.ShapeDtypeStruct((n, 2 * QK_WIDTH + ATTN_WIDTH), BF16),
        ],
        scratch_shapes=[pltpu.VMEM((tm, D_MODEL), BF16)],
        compiler_params=_cparams(("arbitrary",)),
        name="norm_inproj",
    )(xa, xb, g, w_bf, bd, ct, sa, sb)


def _attn_kernel(lq_ref, q_ref, k_ref, v_ref, g_ref, o_ref, vt_scr, sa_scr, sb_scr, p_scr, acc_scr, *,
                 seq, lam_init):
    tq, tk = ATTN_Q_TILE, ATTN_K_TILE
    ks = min(seq, ATTN_K_SUPER)
    nsb, nkc = seq // ks, ks // tk
    groups = tk // SUBLANES
    n_units = (seq // tq) * nsb
    assert n_units % 2 == 0
    s_bufs = (sa_scr, sb_scr)
    for sb in range(nsb):
        for j in range(nkc):
            r0 = sb * ks + j * tk
            vt_scr[sb, :, j * tk:(j + 1) * tk] = v_ref[r0:r0 + tk, :].astype(F32).T.astype(BF16)

    lq = lq_ref[...]
    lam = (jnp.exp(jnp.sum(lq[0:1, :] * lq[1:2, :], axis=1, keepdims=True))
           - jnp.exp(jnp.sum(lq[2:3, :] * lq[3:4, :], axis=1, keepdims=True)) + lam_init)
    gain = g_ref[...]
    first = lax.broadcasted_iota(jnp.int32, (1, DIFF_V_DIM), 1) < DIFF_HEAD_DIM
    nt_dims = (((1,), (1,)), ((), ()))
    comps = (0, 1)

    def load_q(u):
        q = q_ref[pl.ds(_aligned((u // nsb) * tq, tq), tq), :]
        return (jnp.where(first, q, jnp.zeros_like(q)), jnp.where(first, jnp.zeros_like(q), q))

    def score_chunk(u, qc, s_scr, j, mx):
        kblk = k_ref[pl.ds(_aligned((u % nsb) * ks + j * tk, tk), tk), :]
        out = []
        for c in comps:
            s = lax.dot_general(kblk, qc[c], nt_dims, preferred_element_type=F32)
            s_scr[c, j] = s
            out.append(jnp.maximum(mx[c], jnp.max(s.reshape(groups, SUBLANES, tq), axis=0)))
        return out

    def exp_chunk(s_scr, j, m_new, ls):
        out = []
        for c in comps:
            p = jnp.exp2(s_scr[c, j] - m_new[c])
            out.append(ls[c] + jnp.sum(p.reshape(groups, SUBLANES, tq), axis=0))
            p_scr[c, j * tk:(j + 1) * tk, :] = p.astype(BF16)
        return out

    def unit(u, slot, st, with_next):
        mx, m, l = st[0:2], st[2:4], st[4:6]
        sb, qi = u % nsb, u // nsb
        m_new = [jnp.maximum(m[c], jnp.max(mx[c], axis=0, keepdims=True)) for c in comps]
        alpha = [jnp.exp2(m[c] - m_new[c]) for c in comps]
        ls = [jnp.zeros((SUBLANES, tq), F32) for _ in comps]
        mx_next = [jnp.full((SUBLANES, tq), -jnp.inf, F32) for _ in comps]
        qc_next = load_q(u + 1) if with_next else None
        for j in range(nkc):
            if with_next:
                mx_next = score_chunk(u + 1, qc_next, s_bufs[1 - slot], j, mx_next)
            ls = exp_chunk(s_bufs[slot], j, m_new, ls)
        l_new = [alpha[c] * l[c] + jnp.sum(ls[c], axis=0, keepdims=True) for c in comps]
        vt = vt_scr[sb]
        for c in comps:
            acc_scr[c] = alpha[c] * acc_scr[c] + jnp.dot(vt, p_scr[c], preferred_element_type=F32)

        last = jnp.asarray(sb == nsb - 1)

        @pl.when(last)
        def _():
            o_t = acc_scr[0] * (1.0 / l_new[0]) - lam * (acc_scr[1] * (1.0 / l_new[1]))
            ms = jnp.mean(o_t * o_t, axis=0, keepdims=True)
            y_t = ((o_t * lax.rsqrt(ms + EPS)) * gain) * (1.0 - lam_init)
            o_ref[pl.ds(_aligned(qi * tq, tq), tq), :] = y_t.T.astype(BF16)
            acc_scr[...] = jnp.zeros_like(acc_scr)

        m_out = [jnp.where(last, -jnp.inf, m_new[c]) for c in comps]
        l_out = [jnp.where(last, 0.0, l_new[c]) for c in comps]
        return (mx_next[0], mx_next[1], m_out[0], m_out[1], l_out[0], l_out[1])

    acc_scr[...] = jnp.zeros_like(acc_scr)
    mx0 = [jnp.full((SUBLANES, tq), -jnp.inf, F32) for _ in comps]
    qc0 = load_q(0)
    for j in range(nkc):
        mx0 = score_chunk(0, qc0, s_bufs[0], j, mx0)
    neg = jnp.full((1, tq), -jnp.inf, F32)
    zero = jnp.zeros((1, tq), F32)
    st = (mx0[0], mx0[1], neg, neg, zero, zero)

    def pair(i, st):
        st = unit(2 * i, 0, st, True)
        return unit(2 * i + 1, 1, st, True)

    st = lax.fori_loop(0, n_units // 2 - 1, pair, st)
    st = unit(n_units - 2, 0, st, True)
    unit(n_units - 1, 1, st, False)


def _attention(qkv, lq, gain_col, row0, nb, seq, lam_init):
    blk0 = row0 // seq
    nh = N_DIFF_HEADS
    tq, tk = ATTN_Q_TILE, ATTN_K_TILE
    ks = min(seq, ATTN_K_SUPER)
    kern = functools.partial(_attn_kernel, seq=seq, lam_init=lam_init)
    return pl.pallas_call(
        kern,
        grid=(nb, nh),
        in_specs=[
            pl.BlockSpec((4, DIFF_HEAD_DIM), lambda b, h: (0, 0)),
            pl.BlockSpec((seq, LANES), lambda b, h: (blk0 + b, h)),
            pl.BlockSpec((seq, LANES), lambda b, h: (blk0 + b, nh + h)),
            pl.BlockSpec((seq, LANES), lambda b, h: (blk0 + b, 2 * nh + h)),
            pl.BlockSpec((DIFF_V_DIM, 1), lambda b, h: (0, 0)),
        ],
        out_specs=pl.BlockSpec((seq, LANES), lambda b, h: (b, h)),
        out_shape=jax.ShapeDtypeStruct((nb * seq, ATTN_WIDTH), BF16),
        scratch_shapes=[
            pltpu.VMEM((seq // ks, DIFF_V_DIM, ks), BF16),
            pltpu.VMEM((2, ks // tk, tk, tq), F32),
            pltpu.VMEM((2, ks // tk, tk, tq), F32),
            pltpu.VMEM((2, ks, tq), BF16),
            pltpu.VMEM((2, DIFF_V_DIM, tq), F32),
        ],
        compiler_params=_cparams(("arbitrary", "arbitrary")),
        name="diff_attention",
    )(lq, qkv, qkv, qkv, gain_col)


def _dft_kernel(c_ref, s_ref, *rest, nk, nbb, scale):
    y_refs, o_ref, acc_scr = rest[:nbb], rest[nbb], rest[nbb + 1]
    k = pl.program_id(2)

    @pl.when(k == 0)
    def _():
        acc_scr[...] = jnp.zeros_like(acc_scr)

    c, s = c_ref[...], s_ref[...]
    for bb in range(nbb):
        y = y_refs[bb][...]
        acc_scr[bb] += (jnp.dot(c, y[:, 0:FOURIER_WIDTH], preferred_element_type=F32)
                        + jnp.dot(s, y[:, FOURIER_WIDTH:2 * FOURIER_WIDTH], preferred_element_type=F32))

    @pl.when(k == nk - 1)
    def _():
        o_ref[...] = (acc_scr[...] * scale).astype(BF16)


def _fourier(yf, cmat, smat, row0, nb, seq):
    tm = min(seq, DFT_ROW_TILE)
    tk = min(seq, DFT_K_TILE)
    ni, nk = seq // tm, seq // tk
    nbb = nb if (nk > 1 and nb <= DFT_MAX_SEQS_PER_STEP) else 1
    yblk0 = row0 // tk

    def y_spec(bb):
        return pl.BlockSpec((tk, 2 * FOURIER_WIDTH), lambda i, g, k: (yblk0 + (g * nbb + bb) * nk + k, 0))

    kern = functools.partial(_dft_kernel, nk=nk, nbb=nbb, scale=1.0 / math.sqrt(seq))
    out = pl.pallas_call(
        kern,
        grid=(ni, nb // nbb, nk),
        in_specs=[
            pl.BlockSpec((tm, tk), lambda i, g, k: (i, k)),
            pl.BlockSpec((tm, tk), lambda i, g, k: (i, k)),
        ] + [y_spec(bb) for bb in range(nbb)],
        out_specs=pl.BlockSpec((nbb, tm, FOURIER_WIDTH), lambda i, g, k: (g, i, 0)),
        out_shape=jax.ShapeDtypeStruct((nb, seq, FOURIER_WIDTH), BF16),
        scratch_shapes=[pltpu.VMEM((nbb, tm, FOURIER_WIDTH), F32)],
        compiler_params=_cparams(("arbitrary", "arbitrary", "arbitrary")),
        name="fourier_dft",
    )(cmat, smat, *([yf] * nbb))
    return out.reshape(nb * seq, FOURIER_WIDTH)


def _mixed_residual(xa_ref, xb_ref, ofa_ref, ofb_ref, oaa_ref, oab_ref, w_ref, g_ref, x_split, o_split):
    y = (jnp.dot(o_split.pick(ofa_ref, ofb_ref), w_ref[0:FOURIER_WIDTH, :], preferred_element_type=F32)
         + jnp.dot(o_split.pick(oaa_ref, oab_ref), w_ref[FOURIER_WIDTH:D_MODEL, :], preferred_element_type=F32))
    x1 = x_split.pick(xa_ref, xb_ref) + y
    ms = jnp.mean(x1 * x1, axis=-1, keepdims=True)
    return x1, (x1 * lax.rsqrt(ms + EPS)) * g_ref[...]


def _outproj_ffn_kernel(xa_ref, xb_ref, ofa_ref, ofb_ref, oaa_ref, oab_ref, w_ref, g_ref, wgu_ref, wd_ref,
                        o_ref, h_scr, *, x_split, o_split):
    x1, h = _mixed_residual(xa_ref, xb_ref, ofa_ref, ofb_ref, oaa_ref, oab_ref, w_ref, g_ref, x_split, o_split)
    o_ref[...] = x1
    h_scr[...] = h.astype(BF16)
    hb = h_scr[...]
    fc = FFN_CHUNK
    for j in range(FF_DENSE // fc):
        g = jnp.dot(hb, wgu_ref[:, j * fc:(j + 1) * fc], preferred_element_type=F32)
        u = jnp.dot(hb, wgu_ref[:, FF_DENSE + j * fc:FF_DENSE + (j + 1) * fc], preferred_element_type=F32)
        o_ref[...] += jnp.dot(_silu_mul(g, u).astype(BF16), wd_ref[j * fc:(j + 1) * fc, :],
                              preferred_element_type=F32)


def _outproj_router_kernel(xa_ref, xb_ref, ofa_ref, ofb_ref, oaa_ref, oab_ref, w_ref, g_ref, rw_ref,
                           x1_ref, h_ref, rt_ref, cnt_ref, carry_scr, *, x_split, o_split):
    x1, h = _mixed_residual(xa_ref, xb_ref, ofa_ref, ofb_ref, oaa_ref, oab_ref, w_ref, g_ref, x_split, o_split)
    x1_ref[...] = x1
    h_ref[...] = h.astype(BF16)
    _route(h, rw_ref, rt_ref, cnt_ref, carry_scr)


def _resident(shape):
    return pl.BlockSpec(shape, lambda i: (0,) * len(shape), pipeline_mode=pl.Buffered(1))


def _mixed_specs(x_split, o_split, tm):
    return (x_split.specs(tm, D_MODEL) + o_split.specs(tm, FOURIER_WIDTH) + o_split.specs(tm, ATTN_WIDTH)
            + [_resident((D_MODEL, D_MODEL)), pl.BlockSpec((1, D_MODEL), lambda i: (0, 0))])


def _outproj_ffn(xa, xb, x_split, o_f, o_a, o_split, n, w_bf, g, wgu_bf, wd_bf):
    tm = TOKEN_TILE
    return pl.pallas_call(
        functools.partial(_outproj_ffn_kernel, x_split=x_split, o_split=o_split),
        grid=(n // tm,),
        in_specs=_mixed_specs(x_split, o_split, tm) + [_resident((D_MODEL, 2 * FF_DENSE)),
                                                      _resident((FF_DENSE, D_MODEL))],
        out_specs=pl.BlockSpec((tm, D_MODEL), lambda i: (i, 0)),
        out_shape=jax.ShapeDtypeStruct((n, D_MODEL), F32),
        scratch_shapes=[pltpu.VMEM((tm, D_MODEL), BF16)],
        compiler_params=_cparams(("arbitrary",)),
        name="outproj_dense_swiglu",
    )(xa, xb, o_f[0], o_f[1], o_a[0], o_a[1], w_bf, g, wgu_bf, wd_bf)


def _outproj_router(xa, xb, x_split, o_f, o_a, o_split, n, w_bf, g, rw_t):
    tm = TOKEN_TILE
    return pl.pallas_call(
        functools.partial(_outproj_router_kernel, x_split=x_split, o_split=o_split),
        grid=(n // tm,),
        in_specs=_mixed_specs(x_split, o_split, tm) + [pl.BlockSpec((N_EXPERTS, D_MODEL), lambda i: (0, 0))],
        out_specs=[
            pl.BlockSpec((tm, D_MODEL), lambda i: (i, 0)),
            pl.BlockSpec((tm, D_MODEL), lambda i: (i, 0)),
            pl.BlockSpec((SUBLANES, tm), lambda i: (0, i)),
            pl.BlockSpec((N_EXPERTS, LANES), lambda i: (0, 0)),
        ],
        out_shape=[
            jax.ShapeDtypeStruct((n, D_MODEL), F32),
            jax.ShapeDtypeStruct((n, D_MODEL), BF16),
            jax.ShapeDtypeStruct((SUBLANES, n), F32),
            jax.ShapeDtypeStruct((N_EXPERTS, LANES), F32),
        ],
        scratch_shapes=[pltpu.VMEM((N_EXPERTS, LANES), F32)],
        compiler_params=_cparams(("arbitrary",)),
        name="outproj_router",
    )(xa, xb, o_f[0], o_f[1], o_a[0], o_a[1], w_bf, g, rw_t)


def _route(h, rw_ref, rt_ref, cnt_ref, carry_scr):
    tm = TOKEN_TILE

    @pl.when(pl.program_id(0) == 0)
    def _():
        carry_scr[...] = jnp.zeros_like(carry_scr)

    h_hi = h.astype(BF16)
    h_lo = (h - h_hi.astype(F32)).astype(BF16)
    rw = rw_ref[...]
    rw_hi = rw.astype(BF16)
    rw_lo = (rw - rw_hi.astype(F32)).astype(BF16)
    nt_dims = (((1,), (1,)), ((), ()))
    dg = lambda a, b: lax.dot_general(a, b, nt_dims, preferred_element_type=F32)
    logits = dg(rw_hi, h_hi) + (dg(rw_hi, h_lo) + dg(rw_lo, h_hi))

    eidx = lax.broadcasted_iota(jnp.int32, (N_EXPERTS, tm), 0)
    m1 = jnp.max(logits, axis=0, keepdims=True)
    i1 = jnp.min(jnp.where(logits == m1, eidx, N_EXPERTS), axis=0, keepdims=True)
    oh1 = eidx == i1
    rest = jnp.where(oh1, -jnp.inf, logits)
    m2 = jnp.max(rest, axis=0, keepdims=True)
    i2 = jnp.min(jnp.where(rest == m2, eidx, N_EXPERTS), axis=0, keepdims=True)
    oh2 = eidx == i2
    e = jnp.exp(m2 - m1)
    g1 = 1.0 / (1.0 + e)
    g2 = e / (1.0 + e)

    onehot = jnp.where(oh1 | oh2, 1.0, 0.0)
    upper = (lax.broadcasted_iota(jnp.int32, (tm, tm), 0)
             < lax.broadcasted_iota(jnp.int32, (tm, tm), 1))
    before = jnp.dot(onehot.astype(BF16), jnp.where(upper, 1.0, 0.0).astype(BF16),
                     preferred_element_type=F32)
    rank = before + carry_scr[:, 0:1]
    r1 = jnp.sum(jnp.where(oh1, rank, 0.0), axis=0, keepdims=True)
    r2 = jnp.sum(jnp.where(oh2, rank, 0.0), axis=0, keepdims=True)
    zero = jnp.zeros_like(g1)
    rt_ref[...] = jnp.concatenate(
        [i1.astype(F32), i2.astype(F32), r1, r2, g1, g2, zero, zero], axis=0)
    total = carry_scr[...] + jnp.sum(onehot, axis=1, keepdims=True)
    carry_scr[...] = total
    cnt_ref[...] = total


def _row_copy(src, src_row, dst, dst_row, sem):
    return pltpu.make_async_copy(src.at[pl.ds(src_row, 1)], dst.at[pl.ds(dst_row, 1)], sem)


def _dispatch_kernel(s1_ref, s2_ref, h_ref, xs_ref, rows_scr, sem):
    tm = TOKEN_TILE
    rows_scr[...] = h_ref[...].astype(F32)

    def issue(t, c):
        _row_copy(rows_scr, t, xs_ref, s1_ref[t], sem).start(priority=0)
        _row_copy(rows_scr, t, xs_ref, s2_ref[t], sem).start(priority=1)
        return c

    lax.fori_loop(0, tm, issue, 0, unroll=ROW_DMA_UNROLL)
    for _ in range(2):
        pltpu.make_async_copy(rows_scr, xs_ref.at[pl.ds(0, tm)], sem).wait()


def _dispatch(slot1, slot2, h, n_slots):
    n = h.shape[0]
    tm = TOKEN_TILE
    smem = lambda: pl.BlockSpec((tm,), lambda i: (i,), memory_space=pltpu.SMEM)
    return pl.pallas_call(
        _dispatch_kernel,
        grid=(n // tm,),
        in_specs=[smem(), smem(), pl.BlockSpec((tm, D_MODEL), lambda i: (i, 0))],
        out_specs=pl.BlockSpec(memory_space=pl.ANY),
        out_shape=jax.ShapeDtypeStruct((n_slots, D_MODEL), F32),
        scratch_shapes=[pltpu.VMEM((tm, D_MODEL), F32), pltpu.SemaphoreType.DMA(())],
        compiler_params=_cparams(("arbitrary",)),
        name="moe_dispatch",
    )(slot1, slot2, h)


def _expert_kernel(te_ref, nt_ref, tv_ref, xs_ref, wg_ref, wu_ref, wd_ref, ys_ref, xb_scr):
    i = pl.program_id(0)
    j = pl.program_id(1)

    @pl.when(i < nt_ref[0])
    def _():
        @pl.when(j == 0)
        def _():
            rows = lax.broadcasted_iota(jnp.int32, (EXPERT_TILE, 1), 0)
            xb_scr[...] = jnp.where(rows < tv_ref[i], xs_ref[...], 0.0).astype(BF16)
            ys_ref[...] = jnp.zeros_like(ys_ref)

        xb = xb_scr[...]
        fc = FFN_CHUNK
        for t in range(EXPERT_CHUNK // fc):
            g = jnp.dot(xb, wg_ref[0, :, t * fc:(t + 1) * fc], preferred_element_type=F32)
            u = jnp.dot(xb, wu_ref[0, :, t * fc:(t + 1) * fc], preferred_element_type=F32)
            ys_ref[...] += jnp.dot(_silu_mul(g, u).astype(BF16), wd_ref[0, t * fc:(t + 1) * fc, :],
                                   preferred_element_type=F32)


def _experts(tile_expert, n_tiles, tile_valid, xs, wgu_bf, wd_bf):
    n_slots = xs.shape[0]
    te, fc = EXPERT_TILE, EXPERT_CHUNK
    nj = FF_EXPERT // fc
    max_tiles = n_slots // te

    def tile(i, nt):
        return jnp.minimum(i, nt[0] - 1)

    def chunk(i, j, nt):
        return jnp.where(i < nt[0], j, nj - 1)

    grid_spec = pltpu.PrefetchScalarGridSpec(
        num_scalar_prefetch=3,
        grid=(max_tiles, nj),
        in_specs=[
            pl.BlockSpec((te, D_MODEL), lambda i, j, e, nt, tv: (tile(i, nt), 0)),
            pl.BlockSpec((1, D_MODEL, fc), lambda i, j, e, nt, tv: (e[tile(i, nt)], 0, chunk(i, j, nt))),
            pl.BlockSpec((1, D_MODEL, fc), lambda i, j, e, nt, tv: (e[tile(i, nt)], 0, nj + chunk(i, j, nt))),
            pl.BlockSpec((1, fc, D_MODEL), lambda i, j, e, nt, tv: (e[tile(i, nt)], chunk(i, j, nt), 0)),
        ],
        out_specs=pl.BlockSpec((te, D_MODEL), lambda i, j, e, nt, tv: (tile(i, nt), 0)),
        scratch_shapes=[pltpu.VMEM((te, D_MODEL), BF16)],
    )
    return pl.pallas_call(
        _expert_kernel,
        grid_spec=grid_spec,
        out_shape=jax.ShapeDtypeStruct((n_slots, D_MODEL), F32),
        compiler_params=_cparams(("arbitrary", "arbitrary")),
        name="expert_swiglu",
    )(tile_expert, n_tiles, tile_valid, xs, wgu_bf, wgu_bf, wd_bf)


def _combine_kernel(s1_ref, s2_ref, rt_ref, x1_ref, g_ref, ys_ref, op_ref, os_ref, a_scr, b_scr, y_scr, sems, *,
                    tiles_p):
    tm = TOKEN_TILE
    half = tm // 2
    i = pl.program_id(0)

    for hh in range(2):
        def issue(t, c, sem=sems.at[hh]):
            _row_copy(ys_ref, s1_ref[t], a_scr, t, sem).start(priority=0)
            _row_copy(ys_ref, s2_ref[t], b_scr, t, sem).start(priority=1)
            return c

        lax.fori_loop(hh * half, (hh + 1) * half, issue, 0, unroll=ROW_DMA_UNROLL)

    rt = jnp.concatenate([rt_ref[...], jnp.zeros((LANES - SUBLANES, tm), F32)], axis=0)
    rt_t = rt.T

    for hh in range(2):
        rows = pl.ds(hh * half, half)
        pltpu.make_async_copy(ys_ref.at[pl.ds(0, half)], a_scr.at[rows], sems.at[hh]).wait()
        pltpu.make_async_copy(ys_ref.at[pl.ds(0, half)], b_scr.at[rows], sems.at[hh]).wait()
        g1 = rt_t[hh * half:(hh + 1) * half, 4:5]
        g2 = rt_t[hh * half:(hh + 1) * half, 5:6]
        x2 = x1_ref[rows, :] + (g1 * a_scr[rows, :] + g2 * b_scr[rows, :])
        ms = jnp.mean(x2 * x2, axis=-1, keepdims=True)
        y_scr[rows, :] = (x2 * lax.rsqrt(ms + EPS)) * g_ref[...]

    @pl.when(i < tiles_p)
    def _():
        op_ref[...] = y_scr[...]

    @pl.when(i >= tiles_p)
    def _():
        os_ref[...] = y_scr[...]


def _combine(slot1, slot2, rt, x1, g_final, ys, n_p):
    n = x1.shape[0]
    tm = TOKEN_TILE
    tiles_p = n_p // tm
    smem = lambda: pl.BlockSpec((tm,), lambda i: (i,), memory_space=pltpu.SMEM)
    return pl.pallas_call(
        functools.partial(_combine_kernel, tiles_p=tiles_p),
        grid=(n // tm,),
        in_specs=[
            smem(), smem(),
            pl.BlockSpec((SUBLANES, tm), lambda i: (0, i)),
            pl.BlockSpec((tm, D_MODEL), lambda i: (i, 0)),
            pl.BlockSpec((1, D_MODEL), lambda i: (0, 0)),
            pl.BlockSpec(memory_space=pl.ANY),
        ],
        out_specs=[
            pl.BlockSpec((tm, D_MODEL), lambda i: (jnp.minimum(i, tiles_p - 1), 0)),
            pl.BlockSpec((tm, D_MODEL), lambda i: (jnp.maximum(i - tiles_p, 0), 0)),
        ],
        out_shape=[
            jax.ShapeDtypeStruct((n_p, D_MODEL), F32),
            jax.ShapeDtypeStruct((n - n_p, D_MODEL), F32),
        ],
        scratch_shapes=[
            pltpu.VMEM((tm, D_MODEL), F32),
            pltpu.VMEM((tm, D_MODEL), F32),
            pltpu.VMEM((tm, D_MODEL), F32),
            pltpu.SemaphoreType.DMA((2,)),
        ],
        compiler_params=_cparams(("arbitrary",)),
        name="moe_combine_norm",
    )(slot1, slot2, rt, x1, g_final, ys)


def _rope_tables(seq_max):
    half = ROT_DIM // 2
    inv = ROPE_THETA ** (-jnp.arange(0, ROT_DIM, 2, dtype=F32) / ROT_DIM)
    ang = jnp.arange(seq_max, dtype=F32)[:, None] * inv[None, :]
    cos, sin = jnp.cos(ang), jnp.sin(ang)
    ones = jnp.ones((seq_max, DIFF_HEAD_DIM - ROT_DIM), F32)
    zeros = jnp.zeros((seq_max, DIFF_HEAD_DIM - ROT_DIM), F32)
    z8 = jnp.zeros((seq_max, half), F32)
    ct = jnp.concatenate([cos, cos, ones], axis=1)
    sa = jnp.concatenate([-sin, z8, zeros], axis=1)
    sb = jnp.concatenate([z8, sin, zeros], axis=1)
    reps = 2 * LANES // DIFF_HEAD_DIM
    return tuple(jnp.tile(t, (1, reps)) for t in (ct, sa, sb))


def _channel_dft_matrix():
    c = np.arange(FOURIER_GROUP_DIM)
    ang = 2.0 * np.pi * ((c[:, None] * c[None, :]) % FOURIER_GROUP_DIM) / FOURIER_GROUP_DIM
    scale = FOURIER_GROUP_DIM ** -0.5
    eye = np.eye(FOURIER_GROUPS)
    bd = np.concatenate([np.kron(eye, np.cos(ang) * scale), np.kron(eye, np.sin(ang) * scale)], axis=1)
    return jnp.asarray(bd, dtype=BF16)


def _position_dft_matrices(seq):
    lo = FOURIER_GROUP_DIM
    hi = seq // lo
    sp = jnp.arange(seq, dtype=jnp.int32)[None, :]
    a = jnp.arange(hi, dtype=jnp.int32)[:, None]
    b = jnp.arange(lo, dtype=jnp.int32)[:, None]
    w = 2.0 * math.pi / seq
    ang_a = ((a * lo * sp) % seq).astype(F32) * w
    ang_b = ((b * sp) % seq).astype(F32) * w
    ca, sa = jnp.cos(ang_a)[:, None, :], jnp.sin(ang_a)[:, None, :]
    cb, sb = jnp.cos(ang_b)[None, :, :], jnp.sin(ang_b)[None, :, :]
    cmat = (ca * cb - sa * sb).reshape(seq, seq).astype(BF16)
    smat = (-(sa * cb + ca * sb)).reshape(seq, seq).astype(BF16)
    return cmat, smat


def kernel(x_prompt, x_sample, norm_mix, w_in, lambda_qk, subln_gain, w_out, norm_ffn, ffn_w_gate_up,
           ffn_w_down, router_w, expert_w_gate_up, expert_w_down, final_norm):
    nb_p, seq_p, _ = x_prompt.shape
    nb_s, seq_s, _ = x_sample.shape
    n_p, n_s = nb_p * seq_p, nb_s * seq_s
    n = n_p + n_s
    tm = TOKEN_TILE
    assert seq_p % tm == 0 and seq_s % tm == 0
    assert n_p % seq_s == 0 and seq_p % ATTN_K_TILE == 0 and seq_s % ATTN_K_TILE == 0
    assert n_p % min(seq_s, DFT_K_TILE) == 0
    assert DEPTH == 2

    ct, sa, sb = _rope_tables(max(seq_p, seq_s))
    bd = _channel_dft_matrix()
    dft_p = _position_dft_matrices(seq_p)
    dft_s = dft_p if seq_s == seq_p else _position_dft_matrices(seq_s)

    tiles_p = n_p // tm
    two_arrays = _Split(tiles_p, 0)
    one_array = _Split(tiles_p, tiles_p)
    xa, xb, x_split = x_prompt.reshape(n_p, D_MODEL), x_sample.reshape(n_s, D_MODEL), two_arrays
    out = None
    for layer in range(DEPTH):
        lam_init = 0.8 - 0.6 * math.exp(-0.3 * layer)
        yf, qkv = _norm_inproj(xa, xb, x_split, n, norm_mix[layer][None, :], w_in[layer].astype(BF16),
                               bd, ct, sa, sb, seq_p, seq_s)
        gain_col = subln_gain[layer][:, None]
        o_a = (_attention(qkv, lambda_qk[layer], gain_col, 0, nb_p, seq_p, lam_init),
               _attention(qkv, lambda_qk[layer], gain_col, n_p, nb_s, seq_s, lam_init))
        o_f = (_fourier(yf, dft_p[0], dft_p[1], 0, nb_p, seq_p),
               _fourier(yf, dft_s[0], dft_s[1], n_p, nb_s, seq_s))
        mixed = (xa, xb, x_split, o_f, o_a, two_arrays, n, w_out[layer].astype(BF16), norm_ffn[layer][None, :])
        if layer % 2 == 0:
            x = _outproj_ffn(*mixed, ffn_w_gate_up[layer // 2].astype(BF16), ffn_w_down[layer // 2].astype(BF16))
            xa, xb, x_split = x, x, one_array
        else:
            j = layer // 2
            x1, h, rt, cnt = _outproj_router(*mixed, router_w[j].T)
            te = EXPERT_TILE
            counts = cnt[:, 0].astype(jnp.int32)
            padded = ((counts + te - 1) // te) * te
            ends = jnp.cumsum(padded)
            offs = ends - padded
            e1, e2 = rt[0].astype(jnp.int32), rt[1].astype(jnp.int32)
            slot1 = offs[e1] + rt[2].astype(jnp.int32)
            slot2 = offs[e2] + rt[3].astype(jnp.int32)
            n_slots = 2 * n + N_EXPERTS * te
            starts = jnp.arange(n_slots // te, dtype=jnp.int32) * te
            tile_expert = jnp.minimum(jnp.sum((starts[:, None] >= ends[None, :]).astype(jnp.int32), axis=1),
                                      N_EXPERTS - 1)
            tile_valid = jnp.clip(counts[tile_expert] - (starts - offs[tile_expert]), 0, te).astype(jnp.int32)
            n_tiles = (ends[-1:] // te).astype(jnp.int32)
            xs = _dispatch(slot1, slot2, h, n_slots)
            ys = _experts(tile_expert, n_tiles, tile_valid, xs,
                          expert_w_gate_up[j].astype(BF16), expert_w_down[j].astype(BF16))
            out = _combine(slot1, slot2, rt, x1, final_norm[None, :], ys, n_p)
    y_p, y_s = out
    return (y_p.reshape(nb_p, seq_p, D_MODEL), y_s.reshape(nb_s, seq_s, D_MODEL))
```
